```python
import jax, jax.numpy as jnp
from jax import lax
import numpy as np

D_MODEL = 2048
BATCH = 8
SEQ = 4096
DEPTH = 4
DEC_BATCH = 4
DEC_SEQ = 4096
PAST_LEN = 128

N_MIXERS = 2
FOURIER_GROUPS = 4
HEAD_DIM = 128
HEADS_PER_GROUP = 8
DILATION_GROUPS = ((128, 1), (512, 4), (2048, 16))
N_ATTN_GROUPS = 3
QKV_WIDTH = 3 * N_ATTN_GROUPS * HEADS_PER_GROUP * HEAD_DIM
ATTN_OUT_WIDTH = HEADS_PER_GROUP * HEAD_DIM
ROT_DIM = HEAD_DIM // 4
ROPE_THETA = 500000.0
N_EXPERTS = 16
EC_CAPACITY_FACTOR = 2
EXPERT_FF = 2 * D_MODEL
NORM_EPS = 1e-6
NEG_INF = -1e30
N_FOURIER_LAYERS = (DEPTH + N_MIXERS - 1) // N_MIXERS
N_ATTN_LAYERS = DEPTH // N_MIXERS

kernel_name = "hybrid_fnet_dilated_attn_ec_moe_encoder"


def rms_norm(x, g):
    xf = x.astype(jnp.float32)
    y = xf * lax.rsqrt(jnp.mean(xf * xf, axis=-1, keepdims=True) + NORM_EPS)
    return (y * g.astype(jnp.float32)).astype(x.dtype)


def modulate(h, shift, scale):
    return h * (1 + scale[:, None, :]) + shift[:, None, :]


def fourier_mixer(h, w_out):
    B, S, D = h.shape
    hg = h.astype(jnp.float32).reshape(B, S, FOURIER_GROUPS, D // FOURIER_GROUPS)
    mixed = jnp.fft.fft2(hg, axes=(1, 3), norm="ortho").real
    return mixed.reshape(B, S, D).astype(h.dtype) @ w_out


def partial_rotary(x, positions):
    half = ROT_DIM // 2
    inv_freq = ROPE_THETA ** (-jnp.arange(half, dtype=jnp.float32) / half)
    ang = positions.astype(jnp.float32)[:, None] * inv_freq[None, :]
    cos = jnp.cos(ang)[None, :, None, :]
    sin = jnp.sin(ang)[None, :, None, :]
    xr = x[..., :ROT_DIM].astype(jnp.float32)
    x1, x2 = xr[..., :half], xr[..., half:]
    rot = jnp.concatenate([x1 * cos - x2 * sin, x2 * cos + x1 * sin], axis=-1).astype(x.dtype)
    return jnp.concatenate([rot, x[..., ROT_DIM:]], axis=-1)


def dilated_window_attention(q, k, v, dilation, half):
    B, S, H, hd = q.shape
    L = S // dilation
    qb_size = half
    nb = -(-L // qb_size)
    Lp = nb * qb_size

    def to_classes(t):
        t = t.reshape(B, L, dilation, H, hd).transpose(0, 2, 3, 1, 4)
        return jnp.pad(t, ((0, 0), (0, 0), (0, 0), (0, Lp - L), (0, 0)))

    def key_blocks(t):
        tp = jnp.pad(to_classes(t), ((0, 0), (0, 0), (0, 0), (qb_size, qb_size), (0, 0)))
        tb = tp.reshape(B, dilation, H, nb + 2, qb_size, hd)
        return jnp.concatenate([tb[:, :, :, :-2], tb[:, :, :, 1:-1], tb[:, :, :, 2:]], axis=4)

    qb = to_classes(q).reshape(B, dilation, H, nb, qb_size, hd)
    kb = key_blocks(k)
    vb = key_blocks(v)
    scale = 1.0 / float(np.sqrt(hd))
    s = jnp.einsum('bghnqd,bghnkd->bghnqk', qb, kb).astype(jnp.float32) * scale
    blk = jnp.arange(nb)[:, None] * qb_size
    qpos = blk + jnp.arange(qb_size)[None, :]
    kpos = blk - qb_size + jnp.arange(3 * qb_size)[None, :]
    mask = ((jnp.abs(qpos[:, :, None] - kpos[:, None, :]) <= half)
            & (kpos >= 0)[:, None, :] & (kpos < L)[:, None, :])
    s = jnp.where(mask, s, NEG_INF)
    m = jnp.max(s, axis=-1, keepdims=True)
    p = jnp.exp(s - m)
    denom = jnp.sum(p, axis=-1)
    o = jnp.einsum('bghnqk,bghnkd->bghnqd', p, vb.astype(jnp.float32)) / denom[..., None]
    lse = m[..., 0] + jnp.log(denom)
    o = o.reshape(B, dilation, H, Lp, hd)[:, :, :, :L].transpose(0, 3, 1, 2, 4).reshape(B, S, H, hd)
    lse = lse.reshape(B, dilation, H, Lp)[:, :, :, :L].transpose(0, 3, 1, 2).reshape(B, S, H)
    return o, lse


def dilated_attention_mixer(h, w_qkv, w_out):
    B, S, D = h.shape
    qkv = (h @ w_qkv).reshape(B, S, 3, N_ATTN_GROUPS, HEADS_PER_GROUP, HEAD_DIM)
    pos = jnp.arange(S)
    outs, lses = [], []
    for g, (window, dilation) in enumerate(DILATION_GROUPS):
        q = partial_rotary(qkv[:, :, 0, g], pos)
        k = partial_rotary(qkv[:, :, 1, g], pos)
        v = qkv[:, :, 2, g]
        o, lse = dilated_window_attention(q, k, v, dilation, window // (2 * dilation))
        outs.append(o)
        lses.append(lse)
    wts = jax.nn.softmax(jnp.stack(lses, axis=0), axis=0)
    o = jnp.einsum('gbsh,gbshd->bshd', wts, jnp.stack(outs, axis=0))
    return o.reshape(B, S, ATTN_OUT_WIDTH).astype(h.dtype) @ w_out


def ec_moe(h, w_router, w_gate, w_up, w_down):
    B, S, D = h.shape
    T = B * S
    hf = h.reshape(T, D)
    aff = jax.nn.softmax((hf @ w_router).astype(jnp.float32), axis=-1)
    cap = EC_CAPACITY_FACTOR * T // N_EXPERTS
    gates, idx = lax.top_k(aff.T, cap)
    xe = hf[idx]
    hid = jax.nn.silu(jnp.einsum('ecd,edf->ecf', xe, w_gate)) * jnp.einsum('ecd,edf->ecf', xe, w_up)
    ye = jnp.einsum('ecf,efd->ecd', hid, w_down) * gates[..., None].astype(h.dtype)
    out = jnp.zeros_like(hf).at[idx.reshape(-1)].add(ye.reshape(-1, D))
    return out.reshape(B, S, D)


def run_trunk(x, c, g_mix, g_ffn, w_ada, b_ada, w_fourier_out, w_qkv, w_attn_out,
              w_router, w_gate, w_up, w_down, g_final):
    for i in range(DEPTH):
        mod = jax.nn.silu(c) @ w_ada[i] + b_ada[i]
        sh1, sc1, gt1, sh2, sc2, gt2 = jnp.split(mod, 6, axis=-1)
        h = modulate(rms_norm(x, g_mix[i]), sh1, sc1)
        if i % N_MIXERS == 0:
            y = fourier_mixer(h, w_fourier_out[i // N_MIXERS])
        else:
            y = dilated_attention_mixer(h, w_qkv[i // N_MIXERS], w_attn_out[i // N_MIXERS])
        x = x + gt1[:, None, :] * y
        h = modulate(rms_norm(x, g_ffn[i]), sh2, sc2)
        x = x + gt2[:, None, :] * ec_moe(h, w_router[i], w_gate[i], w_up[i], w_down[i])
    return rms_norm(x, g_final)


def setup_inputs(seed: int = 0) -> dict:
    key = jax.random.key(seed)
    ks = jax.random.split(key, 16)
    D = D_MODEL
    nrm = jax.random.normal
    f32 = jnp.float32
    return {
        "x_prompt": nrm(ks[0], (BATCH, SEQ, D), f32),
        "x_sample": nrm(ks[1], (DEC_BATCH, DEC_SEQ, D), f32),
        "c_prompt": nrm(ks[2], (BATCH, D), f32),
        "c_sample": nrm(ks[3], (DEC_BATCH, D), f32),
        "g_mix": 1.0 + 0.05 * nrm(ks[4], (DEPTH, D), f32),
        "g_ffn": 1.0 + 0.05 * nrm(ks[5], (DEPTH, D), f32),
        "w_ada": nrm(ks[6], (DEPTH, D, 6 * D), f32) * (0.5 * D ** -0.5),
        "b_ada": 0.01 * nrm(ks[7], (DEPTH, 6 * D), f32),
        "w_fourier_out": nrm(ks[8], (N_FOURIER_LAYERS, D, D), f32) * D ** -0.5,
        "w_qkv": nrm(ks[9], (N_ATTN_LAYERS, D, QKV_WIDTH), f32) * D ** -0.5,
        "w_attn_out": nrm(ks[10], (N_ATTN_LAYERS, ATTN_OUT_WIDTH, D), f32) * ATTN_OUT_WIDTH ** -0.5,
        "w_router": nrm(ks[11], (DEPTH, D, N_EXPERTS), f32) * D ** -0.5,
        "w_gate": nrm(ks[12], (DEPTH, N_EXPERTS, D, EXPERT_FF), f32) * D ** -0.5,
        "w_up": nrm(ks[13], (DEPTH, N_EXPERTS, D, EXPERT_FF), f32) * D ** -0.5,
        "w_down": nrm(ks[14], (DEPTH, N_EXPERTS, EXPERT_FF, D), f32) * EXPERT_FF ** -0.5,
        "g_final": 1.0 + 0.05 * nrm(ks[15], (D,), f32),
    }


def reference(x_prompt, x_sample, c_prompt, c_sample, g_mix, g_ffn, w_ada, b_ada,
              w_fourier_out, w_qkv, w_attn_out, w_router, w_gate, w_up, w_down, g_final):
    y_prompt = run_trunk(x_prompt, c_prompt, g_mix, g_ffn, w_ada, b_ada, w_fourier_out, w_qkv,
                         w_attn_out, w_router, w_gate, w_up, w_down, g_final)
    y_sample = run_trunk(x_sample, c_sample, g_mix, g_ffn, w_ada, b_ada, w_fourier_out, w_qkv,
                         w_attn_out, w_router, w_gate, w_up, w_down, g_final)
    return (y_prompt, y_sample)
```

```python
import functools
import math

import jax
import jax.numpy as jnp
from jax import lax
from jax.experimental import pallas as pl
from jax.experimental.pallas import tpu as pltpu

BF16 = jnp.bfloat16
F32 = jnp.float32

N_MIXERS = 2
FOURIER_GROUPS = 4
HEAD_DIM = 128
HEADS_PER_GROUP = 8
DILATION_GROUPS = ((128, 1), (512, 4), (2048, 16))
N_ATTN_GROUPS = 3
ROT_DIM = HEAD_DIM // 4
ROPE_THETA = 500000.0
N_EXPERTS = 16
EC_CAPACITY_FACTOR = 2
NORM_EPS = 1e-6
NEG_INF = -1e30

V7X_VMEM_BYTES = 64 * 1024 * 1024
VMEM_LIMIT = V7X_VMEM_BYTES - 8 * 1024 * 1024
LANES = 128
LSE_REP = LANES // HEADS_PER_GROUP


def _params(n_axes):
    return pltpu.CompilerParams(dimension_semantics=("arbitrary",) * n_axes,
                                vmem_limit_bytes=VMEM_LIMIT)


def _tile(n, want):
    t = min(n, want)
    assert n % t == 0, (n, want)
    return t


def _mm_kernel(a_ref, w_ref, o_ref):
    o_ref[...] = jnp.dot(a_ref[...], w_ref[...].astype(BF16),
                         preferred_element_type=F32).astype(o_ref.dtype)


def _mm_bias_kernel(a_ref, w_ref, b_ref, o_ref):
    acc = jnp.dot(a_ref[...], w_ref[...].astype(BF16), preferred_element_type=F32)
    o_ref[...] = (acc + b_ref[...]).astype(o_ref.dtype)


def _mm_residual_kernel(a_ref, w_ref, x_ref, g_ref, o_ref):
    acc = jnp.dot(a_ref[...], w_ref[...].astype(BF16), preferred_element_type=F32)
    o_ref[...] = x_ref[...] + g_ref[0] * acc


def matmul_bias(a, w, b, *, tn):
    M, K = a.shape
    N = w.shape[1]
    tn = _tile(N, tn)
    return pl.pallas_call(
        _mm_bias_kernel,
        grid=(N // tn,),
        in_specs=[pl.BlockSpec((M, K), lambda j: (0, 0)),
                  pl.BlockSpec((K, tn), lambda j: (0, j)),
                  pl.BlockSpec((1, tn), lambda j: (0, j))],
        out_specs=pl.BlockSpec((M, tn), lambda j: (0, j)),
        out_shape=jax.ShapeDtypeStruct((M, N), F32),
        compiler_params=_params(1),
    )(a, w, b)


def matmul_residual(a, w, x, gate, *, rows_per_seq, tm, tn):
    T, K = a.shape
    N = w.shape[1]
    tm = _tile(rows_per_seq, tm)
    tn = _tile(N, tn)
    per_seq = rows_per_seq // tm
    return pl.pallas_call(
        _mm_residual_kernel,
        grid=(T // tm, N // tn),
        in_specs=[pl.BlockSpec((tm, K), lambda i, j: (i, 0)),
                  pl.BlockSpec((K, tn), lambda i, j: (0, j)),
                  pl.BlockSpec((tm, tn), lambda i, j: (i, j)),
                  pl.BlockSpec((1, 1, tn), lambda i, j: (i // per_seq, 0, j))],
        out_specs=pl.BlockSpec((tm, tn), lambda i, j: (i, j)),
        out_shape=jax.ShapeDtypeStruct((T, N), F32),
        compiler_params=_params(2),
    )(a, w, x, gate)


def _qkv_rope_kernel(a_ref, w_ref, cos_ref, sa_ref, sb_ref, o_ref, *, n_rot_tiles):
    acc = jnp.dot(a_ref[...], w_ref[...].astype(BF16), preferred_element_type=F32)
    j = pl.program_id(1)

    @pl.when(j < n_rot_tiles)
    def _():
        cos, sa, sb = cos_ref[...], sa_ref[...], sb_ref[...]
        half = ROT_DIM // 2
        for h in range(acc.shape[1] // HEAD_DIM):
            xh = acc[:, h * HEAD_DIM:(h + 1) * HEAD_DIM]
            up = pltpu.roll(xh, HEAD_DIM - half, 1)
            dn = pltpu.roll(xh, half, 1)
            o_ref[:, h * HEAD_DIM:(h + 1) * HEAD_DIM] = (xh * cos + up * sa + dn * sb).astype(o_ref.dtype)

    @pl.when(j >= n_rot_tiles)
    def _():
        o_ref[...] = acc.astype(o_ref.dtype)


def matmul_qkv_rope(h, w_qkv, group, cos_t, sin_a, sin_b, *, rows_per_seq, tm, tn):
    T, K = h.shape
    gw = HEADS_PER_GROUP * HEAD_DIM
    tm = _tile(rows_per_seq, tm)
    tn = _tile(gw, tn)
    per_seq = rows_per_seq // tm
    n_per = gw // tn
    tab = pl.BlockSpec((tm, HEAD_DIM), lambda i, j: (i % per_seq, 0))
    return pl.pallas_call(
        functools.partial(_qkv_rope_kernel, n_rot_tiles=2 * n_per),
        grid=(T // tm, 3 * n_per),
        in_specs=[pl.BlockSpec((tm, K), lambda i, j: (i, 0)),
                  pl.BlockSpec((K, tn), lambda i, j: (0, (j // n_per) * N_ATTN_GROUPS * n_per
                                                      + group * n_per + j % n_per)),
                  tab, tab, tab],
        out_specs=pl.BlockSpec((tm, tn), lambda i, j: (i, j)),
        out_shape=jax.ShapeDtypeStruct((T, 3 * gw), BF16),
        compiler_params=_params(2),
    )(h, w_qkv, cos_t, sin_a, sin_b)


def _norm_mod_kernel(x_ref, g_ref, sh_ref, sc_ref, o_ref):
    x = x_ref[...]
    y = x * lax.rsqrt(jnp.mean(x * x, axis=-1, keepdims=True) + NORM_EPS) * g_ref[...]
    o_ref[...] = (y * (1.0 + sc_ref[0]) + sh_ref[0]).astype(o_ref.dtype)


def _norm_mod_router_kernel(x_ref, g_ref, sh_ref, sc_ref, wr_ref, o_ref, lg_ref):
    x = x_ref[...]
    y = x * lax.rsqrt(jnp.mean(x * x, axis=-1, keepdims=True) + NORM_EPS) * g_ref[...]
    h = (y * (1.0 + sc_ref[0]) + sh_ref[0]).astype(BF16)
    o_ref[...] = h
    lg_ref[...] = jnp.dot(h, wr_ref[...], preferred_element_type=F32)


def norm_modulate(x, g, shift, scale, *, rows_per_seq, ts, w_router=None):
    T, D = x.shape
    ts = _tile(rows_per_seq, ts)
    per_seq = rows_per_seq // ts
    row = pl.BlockSpec((ts, D), lambda i: (i, 0))
    vec = pl.BlockSpec((1, D), lambda i: (0, 0))
    seq = pl.BlockSpec((1, 1, D), lambda i: (i // per_seq, 0, 0))
    if w_router is None:
        return pl.pallas_call(
            _norm_mod_kernel, grid=(T // ts,),
            in_specs=[row, vec, seq, seq], out_specs=row,
            out_shape=jax.ShapeDtypeStruct((T, D), BF16),
            compiler_params=_params(1),
        )(x, g, shift, scale)
    return pl.pallas_call(
        _norm_mod_router_kernel, grid=(T // ts,),
        in_specs=[row, vec, seq, seq, pl.BlockSpec((D, LANES), lambda i: (0, 0))],
        out_specs=[row, pl.BlockSpec((ts, LANES), lambda i: (i, 0))],
        out_shape=[jax.ShapeDtypeStruct((T, D), BF16), jax.ShapeDtypeStruct((T, LANES), F32)],
        compiler_params=_params(1),
    )(x, g, shift, scale, w_router)


def _final_norm_kernel(x_ref, g_ref, o_ref):
    x = x_ref[...]
    o_ref[...] = x * lax.rsqrt(jnp.mean(x * x, axis=-1, keepdims=True) + NORM_EPS) * g_ref[...]


def final_norm(x, g, *, ts):
    T, D = x.shape
    ts = _tile(T, ts)
    return pl.pallas_call(
        _final_norm_kernel, grid=(T // ts,),
        in_specs=[pl.BlockSpec((ts, D), lambda i: (i, 0)), pl.BlockSpec((1, D), lambda i: (0, 0))],
        out_specs=pl.BlockSpec((ts, D), lambda i: (i, 0)),
        out_shape=jax.ShapeDtypeStruct((T, D), F32),
        compiler_params=_params(1),
    )(x, g)


def dft_tables(S, C):
    r = int(round(math.sqrt(S)))
    assert r * r == S
    k = jnp.arange(S, dtype=jnp.int32)[:, None]
    n = jnp.arange(r, dtype=jnp.int32)[None, :]
    ang_hi = ((k * n * r) % S).astype(F32) * (2.0 * math.pi / S)
    ang_lo = ((k * n) % S).astype(F32) * (2.0 * math.pi / S)
    ch, sh_ = jnp.cos(ang_hi)[:, :, None], jnp.sin(ang_hi)[:, :, None]
    cl, sl = jnp.cos(ang_lo)[:, None, :], jnp.sin(ang_lo)[:, None, :]
    scale = float(S) ** -0.5
    cos_s = ((ch * cl - sh_ * sl) * scale).reshape(S, S)
    sin_s = ((sh_ * cl + ch * sl) * scale).reshape(S, S)
    m_seq = jnp.concatenate([cos_s, sin_s], axis=0).astype(BF16)
    kc = jnp.arange(C, dtype=jnp.int32)
    ang_c = ((kc[:, None] * kc[None, :]) % C).astype(F32) * (2.0 * math.pi / C)
    cscale = float(C) ** -0.5
    m_ch = jnp.concatenate([jnp.cos(ang_c) * cscale, -jnp.sin(ang_c) * cscale], axis=0).astype(BF16)
    return m_seq, m_ch


def _dft_seq_kernel(m_ref, h_ref, o_ref):
    o_ref[0] = jnp.dot(m_ref[...], h_ref[0], preferred_element_type=F32).astype(o_ref.dtype)


def dft_seq(m_seq, h3, *, tm):
    B, S, D = h3.shape
    C = D // FOURIER_GROUPS
    tm = _tile(S, tm)
    n_i = S // tm
    return pl.pallas_call(
        _dft_seq_kernel,
        grid=(2 * n_i, B, FOURIER_GROUPS),
        in_specs=[pl.BlockSpec((tm, S), lambda i, b, g: (i, 0)),
                  pl.BlockSpec((1, S, C), lambda i, b, g: (b, 0, g))],
        out_specs=pl.BlockSpec((1, tm, C), lambda i, b, g: (b, i % n_i, 2 * g + i // n_i)),
        out_shape=jax.ShapeDtypeStruct((B, S, 2 * D), BF16),
        compiler_params=_params(3),
    )(m_seq, h3)


def _mm_const_kernel(a_ref, w_ref, o_ref):
    o_ref[...] = jnp.dot(a_ref[...], w_ref[...], preferred_element_type=F32).astype(o_ref.dtype)


def dft_channel(z, m_ch, *, tm):
    T = z.shape[0]
    C2, C = m_ch.shape
    tm = _tile(T, tm)
    return pl.pallas_call(
        _mm_const_kernel,
        grid=(T // tm, FOURIER_GROUPS),
        in_specs=[pl.BlockSpec((tm, C2), lambda i, g: (i, g)),
                  pl.BlockSpec((C2, C), lambda i, g: (0, 0))],
        out_specs=pl.BlockSpec((tm, C), lambda i, g: (i, g)),
        out_shape=jax.ShapeDtypeStruct((T, FOURIER_GROUPS * C), BF16),
        compiler_params=_params(2),
    )(z, m_ch)


def rope_tables(S):
    half = ROT_DIM // 2
    inv_freq = ROPE_THETA ** (-jnp.arange(half, dtype=F32) / half)
    ang = jnp.arange(S, dtype=F32)[:, None] * inv_freq[None, :]
    cos, sin = jnp.cos(ang), jnp.sin(ang)
    pad = HEAD_DIM - ROT_DIM
    cos_t = jnp.concatenate([cos, cos, jnp.ones((S, pad), F32)], axis=1)
    sin_a = jnp.concatenate([-sin, jnp.zeros((S, half + pad), F32)], axis=1)
    sin_b = jnp.concatenate([jnp.zeros((S, half), F32), sin, jnp.zeros((S, pad), F32)], axis=1)
    return cos_t, sin_a, sin_b


def _band_attn_kernel(q_ref, k_ref, v_ref, o_ref, lse_ref, *, L, tq, win, half):
    hh = pl.program_id(2)
    scale = 1.0 / math.sqrt(HEAD_DIM)
    lane_head = lax.broadcasted_iota(jnp.int32, (tq, LANES), 1) // LSE_REP

    @pl.when(hh == 0)
    def _():
        lse_ref[...] = jnp.zeros(lse_ref.shape, lse_ref.dtype)

    def body(qi, carry):
        q0 = pl.multiple_of(qi * tq, tq)
        ks = jnp.clip(q0 - half, 0, L - win)
        ks = pl.multiple_of(ks, half)
        q = q_ref[0, pl.ds(q0, tq), :]
        k = k_ref[0, pl.ds(ks, win), :]
        v = v_ref[0, pl.ds(ks, win), :]
        s = lax.dot_general(q, k, (((1,), (1,)), ((), ())), preferred_element_type=F32) * scale
        qpos = q0 + lax.broadcasted_iota(jnp.int32, (tq, win), 0)
        kpos = ks + lax.broadcasted_iota(jnp.int32, (tq, win), 1)
        s = jnp.where(jnp.abs(qpos - kpos) <= half, s, NEG_INF)
        m = jnp.max(s, axis=-1, keepdims=True)
        p = jnp.exp(s - m)
        denom = jnp.sum(p, axis=-1, keepdims=True)
        o = jnp.dot(p.astype(BF16), v, preferred_element_type=F32) / denom
        o_ref[0, pl.ds(q0, tq), :] = o.astype(o_ref.dtype)
        lse = m + jnp.log(denom)
        prev = lse_ref[0, pl.ds(q0, tq), :]
        lse_ref[0, pl.ds(q0, tq), :] = jnp.where(lane_head == hh, lse, prev)
        return carry

    lax.fori_loop(0, L // tq, body, 0)


def band_attention(qkv, dilation, half, *, B, S):
    width = qkv.shape[1]
    L = S // dilation
    H = HEADS_PER_GROUP
    tq = min(2 * half, L)
    win = min(tq + 2 * half, L)
    blocks_per_row = width // HEAD_DIM
    qv = qkv.reshape(B, L, dilation * width)

    def spec(which):
        return pl.BlockSpec((1, L, HEAD_DIM),
                            lambda b, r, h: (b, 0, r * blocks_per_row + which * H + h))

    o, lse = pl.pallas_call(
        functools.partial(_band_attn_kernel, L=L, tq=tq, win=win, half=half),
        grid=(B, dilation, H),
        in_specs=[spec(0), spec(1), spec(2)],
        out_specs=[pl.BlockSpec((1, L, HEAD_DIM), lambda b, r, h: (b, 0, r * H + h)),
                   pl.BlockSpec((1, L, LANES), lambda b, r, h: (b, 0, r))],
        out_shape=[jax.ShapeDtypeStruct((B, L, dilation * H * HEAD_DIM), F32),
                   jax.ShapeDtypeStruct((B, L, dilation * LANES), F32)],
        compiler_params=_params(3),
    )(qv, qv, qv)
    return o.reshape(B * S, H * HEAD_DIM), lse.reshape(B * S, LANES)


def _attn_merge_kernel(o0_ref, o1_ref, o2_ref, l0_ref, l1_ref, l2_ref, out_ref):
    l0, l1, l2 = l0_ref[...], l1_ref[...], l2_ref[...]
    m = jnp.maximum(jnp.maximum(l0, l1), l2)
    e0, e1, e2 = jnp.exp(l0 - m), jnp.exp(l1 - m), jnp.exp(l2 - m)
    inv = 1.0 / (e0 + e1 + e2)
    w0, w1, w2 = e0 * inv, e1 * inv, e2 * inv
    for h in range(HEADS_PER_GROUP):
        cs = slice(h * HEAD_DIM, (h + 1) * HEAD_DIM)
        c = h * LSE_REP
        acc = (w0[:, c:c + 1] * o0_ref[:, cs] + w1[:, c:c + 1] * o1_ref[:, cs]
               + w2[:, c:c + 1] * o2_ref[:, cs])
        out_ref[:, cs] = acc.astype(out_ref.dtype)


def attn_merge(outs, lses, *, tm):
    T, W = outs[0].shape
    tm = _tile(T, tm)
    o_spec = pl.BlockSpec((tm, W), lambda i: (i, 0))
    l_spec = pl.BlockSpec((tm, LANES), lambda i: (i, 0))
    return pl.pallas_call(
        _attn_merge_kernel, grid=(T // tm,),
        in_specs=[o_spec] * 3 + [l_spec] * 3, out_specs=o_spec,
        out_shape=jax.ShapeDtypeStruct((T, W), BF16),
        compiler_params=_params(1),
    )(*outs, *lses)


def _expert_ffn_kernel(x_ref, wg_ref, wu_ref, wd_ref, gate_ref, o_ref):
    f = pl.program_id(2)
    x = x_ref[0]
    g = jnp.dot(x, wg_ref[0].astype(BF16), preferred_element_type=F32)
    u = jnp.dot(x, wu_ref[0].astype(BF16), preferred_element_type=F32)
    hid = (g * jax.nn.sigmoid(g) * u).astype(BF16)
    part = jnp.dot(hid, wd_ref[0].astype(BF16), preferred_element_type=F32)

    @pl.when(f == 0)
    def _():
        o_ref[0] = part

    @pl.when(f > 0)
    def _():
        o_ref[0] += part

    @pl.when(f == pl.num_programs(2) - 1)
    def _():
        o_ref[0] = o_ref[0] * gate_ref[0]


def expert_ffn(xe, w_gate, w_up, w_down, gates, *, tm, tf):
    E, C, D = xe.shape
    F = w_gate.shape[2]
    tm = _tile(C, tm)
    tf = _tile(F, tf)
    return pl.pallas_call(
        _expert_ffn_kernel,
        grid=(E, C // tm, F // tf),
        in_specs=[pl.BlockSpec((1, tm, D), lambda e, i, f: (e, i, 0)),
                  pl.BlockSpec((1, D, tf), lambda e, i, f: (e, 0, f)),
                  pl.BlockSpec((1, D, tf), lambda e, i, f: (e, 0, f)),
                  pl.BlockSpec((1, tf, D), lambda e, i, f: (e, f, 0)),
                  pl.BlockSpec((1, tm, 1), lambda e, i, f: (e, i, 0))],
        out_specs=pl.BlockSpec((1, tm, D), lambda e, i, f: (e, i, 0)),
        out_shape=jax.ShapeDtypeStruct((E, C, D), F32),
        compiler_params=_params(3),
    )(xe, w_gate, w_up, w_down, gates)


def _ada_modulation(c_all, w_ada, b_ada):
    n = c_all.shape[0]
    pad = (-n) % 16
    a = jnp.pad(jax.nn.silu(c_all), ((0, pad), (0, 0))).astype(BF16)
    mods = [matmul_bias(a, w_ada[i], b_ada[i][None, :], tn=1024)[:n] for i in range(w_ada.shape[0])]
    return mods


def _moe(h, logits, w_gate, w_up, w_down):
    T, D = h.shape
    aff = jax.nn.softmax(logits[:, :N_EXPERTS], axis=-1)
    cap = EC_CAPACITY_FACTOR * T // N_EXPERTS
    gates, idx = lax.top_k(aff.T, cap)
    xe = h[idx]
    ye = expert_ffn(xe, w_gate, w_up, w_down, gates[..., None], tm=1024, tf=256)
    return jnp.zeros((T, D), F32).at[idx.reshape(-1)].add(ye.reshape(-1, D))


def _run_trunk(x, mods, g_mix, g_ffn, w_fourier_out, w_qkv, w_attn_out, w_router_pad,
               w_gate, w_up, w_down, g_final, tables):
    B, S, D = x.shape
    T = B * S
    m_seq, m_ch, rope = tables
    x = x.reshape(T, D)
    depth = g_mix.shape[0]
    for i in range(depth):
        sh1, sc1, gt1, sh2, sc2, gt2 = [t[:, None, :] for t in jnp.split(mods[i], 6, axis=-1)]
        h = norm_modulate(x, g_mix[i][None, :], sh1, sc1, rows_per_seq=S, ts=512)
        if i % N_MIXERS == 0:
            z = dft_seq(m_seq, h.reshape(B, S, D), tm=1024)
            mixed = dft_channel(z.reshape(T, 2 * D), m_ch, tm=1024)
            x = matmul_residual(mixed, w_fourier_out[i // N_MIXERS], x, gt1, rows_per_seq=S, tm=1024, tn=512)
        else:
            outs, lses = [], []
            for g, (window, dilation) in enumerate(DILATION_GROUPS):
                qkv = matmul_qkv_rope(h, w_qkv[i // N_MIXERS], g, *rope, rows_per_seq=S, tm=1024, tn=512)
                o, lse = band_attention(qkv, dilation, window // (2 * dilation), B=B, S=S)
                outs.append(o)
                lses.append(lse)
            merged = attn_merge(outs, lses, tm=512)
            x = matmul_residual(merged, w_attn_out[i // N_MIXERS], x, gt1, rows_per_seq=S, tm=1024, tn=512)
        h, logits = norm_modulate(x, g_ffn[i][None, :], sh2, sc2, rows_per_seq=S, ts=512,
                                  w_router=w_router_pad[i])
        moe = _moe(h, logits, w_gate[i], w_up[i], w_down[i])
        x = (x.reshape(B, S, D) + gt2 * moe.reshape(B, S, D)).reshape(T, D)
    return final_norm(x, g_final[None, :], ts=512).reshape(B, S, D)


def kernel(x_prompt, x_sample, c_prompt, c_sample, g_mix, g_ffn, w_ada, b_ada, w_fourier_out, w_qkv,
           w_attn_out, w_router, w_gate, w_up, w_down, g_final):
    nb = x_prompt.shape[0]
    mods = _ada_modulation(jnp.concatenate([c_prompt, c_sample], axis=0), w_ada, b_ada)
    w_router_pad = jnp.pad(w_router, ((0, 0), (0, 0), (0, LANES - w_router.shape[2]))).astype(BF16)
    outs = []
    for x, lo, hi in ((x_prompt, 0, nb), (x_sample, nb, nb + x_sample.shape[0])):
        S, D = x.shape[1], x.shape[2]
        tables = dft_tables(S, D // FOURIER_GROUPS) + (rope_tables(S),)
        outs.append(_run_trunk(x, [m[lo:hi] for m in mods], g_mix, g_ffn, w_fourier_out, w_qkv,
                               w_attn_out, w_router_pad, w_gate, w_up, w_down, g_final, tables))
    return tuple(outs)
```

```python
import functools
import math

import jax
import jax.numpy as jnp
from jax import lax
from jax.experimental import pallas as pl
from jax.experimental.pallas import tpu as pltpu

BF16 = jnp.bfloat16
F32 = jnp.float32

N_MIXERS = 2
FOURIER_GROUPS = 4
HEAD_DIM = 128
HEADS_PER_GROUP = 8
DILATION_GROUPS = ((128, 1), (512, 4), (2048, 16))
N_ATTN_GROUPS = 3
ROT_DIM = HEAD_DIM // 4
ROPE_THETA = 500000.0
N_EXPERTS = 16
EC_CAPACITY_FACTOR = 2
NORM_EPS = 1e-6
NEG_INF = -1e30

V7X_VMEM_BYTES = 64 * 1024 * 1024
VMEM_LIMIT = V7X_VMEM_BYTES - 8 * 1024 * 1024
LANES = 128
LSE_REP = LANES // HEADS_PER_GROUP


def _params(n_axes):
    return pltpu.CompilerParams(dimension_semantics=("arbitrary",) * n_axes,
                                vmem_limit_bytes=VMEM_LIMIT)


def _tile(n, want):
    t = min(n, want)
    assert n % t == 0, (n, want)
    return t


def _mm_bias_kernel(a_ref, w_ref, b_ref, o_ref):
    acc = jnp.dot(a_ref[...], w_ref[0].astype(BF16), preferred_element_type=F32)
    o_ref[0] = (acc + b_ref[0]).astype(o_ref.dtype)


def _mm_residual_kernel(a_ref, w_ref, x_ref, g_ref, o_ref):
    acc = jnp.dot(a_ref[...], w_ref[0].astype(BF16), preferred_element_type=F32)
    o_ref[...] = x_ref[...] + g_ref[0] * acc


def matmul_bias_layers(a, w, b, *, tn):
    M, K = a.shape
    n_layers, _, N = w.shape
    tn = _tile(N, tn)
    return pl.pallas_call(
        _mm_bias_kernel,
        grid=(n_layers, N // tn),
        in_specs=[pl.BlockSpec((M, K), lambda l, j: (0, 0)),
                  pl.BlockSpec((1, K, tn), lambda l, j: (l, 0, j)),
                  pl.BlockSpec((1, 1, tn), lambda l, j: (l, 0, j))],
        out_specs=pl.BlockSpec((1, M, tn), lambda l, j: (l, 0, j)),
        out_shape=jax.ShapeDtypeStruct((n_layers, M, N), F32),
        compiler_params=_params(2),
    )(a, w, b)


def matmul_residual(a, w, layer, x, gate, *, rows_per_seq, tm, tn):
    T, K = a.shape
    N = w.shape[2]
    tm = _tile(rows_per_seq, tm)
    tn = _tile(N, tn)
    per_seq = rows_per_seq // tm
    return pl.pallas_call(
        _mm_residual_kernel,
        grid=(T // tm, N // tn),
        in_specs=[pl.BlockSpec((tm, K), lambda i, j: (i, 0)),
                  pl.BlockSpec((1, K, tn), lambda i, j: (layer, 0, j)),
                  pl.BlockSpec((tm, tn), lambda i, j: (i, j)),
                  pl.BlockSpec((1, 1, tn), lambda i, j: (i // per_seq, 0, j))],
        out_specs=pl.BlockSpec((tm, tn), lambda i, j: (i, j)),
        out_shape=jax.ShapeDtypeStruct((T, N), F32),
        compiler_params=_params(2),
    )(a, w, x, gate)


def permute_qk_columns(w_qkv):
    n_layers, D, N = w_qkv.shape
    w = w_qkv.reshape(n_layers, D, 3, N // (3 * HEAD_DIM), HEAD_DIM)
    half, mid = ROT_DIM // 2, HEAD_DIM // 2
    qk = w[:, :, :2]
    qk = jnp.concatenate([qk[..., :half], qk[..., ROT_DIM:ROT_DIM + mid - half],
                          qk[..., half:ROT_DIM], qk[..., ROT_DIM + mid - half:]], axis=-1)
    return jnp.concatenate([qk, w[:, :, 2:]], axis=2).reshape(n_layers, D, N)


def rope_tables(S):
    half, mid = ROT_DIM // 2, HEAD_DIM // 2
    inv_freq = ROPE_THETA ** (-jnp.arange(half, dtype=F32) / half)
    ang = jnp.arange(S, dtype=F32)[:, None] * inv_freq[None, :]
    cos, sin = jnp.cos(ang), jnp.sin(ang)
    one = jnp.ones((S, mid - half), F32)
    zero = jnp.zeros((S, mid - half), F32)
    cos_t = jnp.concatenate([cos, one, cos, one], axis=1)
    sin_t = jnp.concatenate([-sin, zero, sin, zero], axis=1)
    cos_all = jnp.stack([cos_t, jnp.ones_like(cos_t)])
    sin_all = jnp.stack([sin_t, jnp.zeros_like(sin_t)])
    return cos_all, sin_all


def _qkv_rope_kernel(a_ref, w_ref, cos_ref, sin_ref, o_ref):
    acc = jnp.dot(a_ref[...], w_ref[0].astype(BF16), preferred_element_type=F32)
    cos, sin = cos_ref[0], sin_ref[0]
    for h in range(acc.shape[1] // HEAD_DIM):
        xh = acc[:, h * HEAD_DIM:(h + 1) * HEAD_DIM]
        rot = pltpu.roll(xh, HEAD_DIM // 2, 1)
        o_ref[:, h * HEAD_DIM:(h + 1) * HEAD_DIM] = (xh * cos + rot * sin).astype(o_ref.dtype)


def matmul_qkv_rope(h, w_qkv, layer, group, cos_all, sin_all, *, rows_per_seq, tm, tn):
    T, K = h.shape
    gw = HEADS_PER_GROUP * HEAD_DIM
    tm = _tile(rows_per_seq, tm)
    tn = _tile(gw, tn)
    per_seq = rows_per_seq // tm
    n_per = gw // tn
    tab = pl.BlockSpec((1, tm, HEAD_DIM), lambda i, j: (j // (2 * n_per), i % per_seq, 0))
    return pl.pallas_call(
        _qkv_rope_kernel,
        grid=(T // tm, 3 * n_per),
        in_specs=[pl.BlockSpec((tm, K), lambda i, j: (i, 0)),
                  pl.BlockSpec((1, K, tn), lambda i, j: (layer, 0, (j // n_per) * N_ATTN_GROUPS * n_per
                                                         + group * n_per + j % n_per)),
                  tab, tab],
        out_specs=pl.BlockSpec((tm, tn), lambda i, j: (i, j)),
        out_shape=jax.ShapeDtypeStruct((T, 3 * gw), BF16),
        compiler_params=_params(2),
    )(h, w_qkv, cos_all, sin_all)


def _norm_mod_kernel(x_ref, g_ref, sh_ref, sc_ref, o_ref):
    x = x_ref[...]
    y = x * lax.rsqrt(jnp.mean(x * x, axis=-1, keepdims=True) + NORM_EPS) * g_ref[...]
    o_ref[...] = (y * (1.0 + sc_ref[0]) + sh_ref[0]).astype(o_ref.dtype)


def _norm_mod_router_kernel(x_ref, g_ref, sh_ref, sc_ref, wr_ref, o_ref, lg_ref):
    x = x_ref[...]
    y = x * lax.rsqrt(jnp.mean(x * x, axis=-1, keepdims=True) + NORM_EPS) * g_ref[...]
    h = (y * (1.0 + sc_ref[0]) + sh_ref[0]).astype(BF16)
    o_ref[...] = h
    lg_ref[...] = jnp.dot(h, wr_ref[...], preferred_element_type=F32)


def norm_modulate(x, g, shift, scale, *, rows_per_seq, ts, w_router=None):
    T, D = x.shape
    ts = _tile(rows_per_seq, ts)
    per_seq = rows_per_seq // ts
    row = pl.BlockSpec((ts, D), lambda i: (i, 0))
    vec = pl.BlockSpec((1, D), lambda i: (0, 0))
    seq = pl.BlockSpec((1, 1, D), lambda i: (i // per_seq, 0, 0))
    if w_router is None:
        return pl.pallas_call(
            _norm_mod_kernel, grid=(T // ts,),
            in_specs=[row, vec, seq, seq], out_specs=row,
            out_shape=jax.ShapeDtypeStruct((T, D), BF16),
            compiler_params=_params(1),
        )(x, g, shift, scale)
    return pl.pallas_call(
        _norm_mod_router_kernel, grid=(T // ts,),
        in_specs=[row, vec, seq, seq, pl.BlockSpec((D, LANES), lambda i: (0, 0))],
        out_specs=[row, pl.BlockSpec((ts, LANES), lambda i: (i, 0))],
        out_shape=[jax.ShapeDtypeStruct((T, D), BF16), jax.ShapeDtypeStruct((T, LANES), F32)],
        compiler_params=_params(1),
    )(x, g, shift, scale, w_router)


def _final_norm_kernel(x_ref, g_ref, o_ref):
    x = x_ref[...]
    o_ref[...] = x * lax.rsqrt(jnp.mean(x * x, axis=-1, keepdims=True) + NORM_EPS) * g_ref[...]


def final_norm(x, g, *, ts):
    T, D = x.shape
    ts = _tile(T, ts)
    return pl.pallas_call(
        _final_norm_kernel, grid=(T // ts,),
        in_specs=[pl.BlockSpec((ts, D), lambda i: (i, 0)), pl.BlockSpec((1, D), lambda i: (0, 0))],
        out_specs=pl.BlockSpec((ts, D), lambda i: (i, 0)),
        out_shape=jax.ShapeDtypeStruct((T, D), F32),
        compiler_params=_params(1),
    )(x, g)


def dft_tables(S, C):
    r = int(round(math.sqrt(S)))
    assert r * r == S
    k = jnp.arange(S, dtype=jnp.int32)[:, None]
    n = jnp.arange(r, dtype=jnp.int32)[None, :]
    ang_hi = ((k * n * r) % S).astype(F32) * (2.0 * math.pi / S)
    ang_lo = ((k * n) % S).astype(F32) * (2.0 * math.pi / S)
    ch, sh_ = jnp.cos(ang_hi)[:, :, None], jnp.sin(ang_hi)[:, :, None]
    cl, sl = jnp.cos(ang_lo)[:, None, :], jnp.sin(ang_lo)[:, None, :]
    scale = float(S) ** -0.5
    cos_s = ((ch * cl - sh_ * sl) * scale).reshape(S, S)
    sin_s = ((sh_ * cl + ch * sl) * scale).reshape(S, S)
    m_seq = jnp.concatenate([cos_s, sin_s], axis=0).astype(BF16)
    kc = jnp.arange(C, dtype=jnp.int32)
    ang_c = ((kc[:, None] * kc[None, :]) % C).astype(F32) * (2.0 * math.pi / C)
    cscale = float(C) ** -0.5
    m_ch = jnp.concatenate([jnp.cos(ang_c) * cscale, -jnp.sin(ang_c) * cscale], axis=0).astype(BF16)
    return m_seq, m_ch


def _dft_seq_kernel(m_ref, h_ref, o_ref):
    o_ref[0] = jnp.dot(m_ref[...], h_ref[0], preferred_element_type=F32).astype(o_ref.dtype)


def dft_seq(m_seq, h3, *, tm):
    B, S, D = h3.shape
    C = D // FOURIER_GROUPS
    tm = _tile(S, tm)
    n_i = S // tm
    return pl.pallas_call(
        _dft_seq_kernel,
        grid=(2 * n_i, B, FOURIER_GROUPS),
        in_specs=[pl.BlockSpec((tm, S), lambda i, b, g: (i, 0)),
                  pl.BlockSpec((1, S, C), lambda i, b, g: (b, 0, g))],
        out_specs=pl.BlockSpec((1, tm, C), lambda i, b, g: (b, i % n_i, 2 * g + i // n_i)),
        out_shape=jax.ShapeDtypeStruct((B, S, 2 * D), BF16),
        compiler_params=_params(3),
    )(m_seq, h3)


def _mm_const_kernel(a_ref, w_ref, o_ref):
    o_ref[...] = jnp.dot(a_ref[...], w_ref[...], preferred_element_type=F32).astype(o_ref.dtype)


def dft_channel(z, m_ch, *, tm):
    T = z.shape[0]
    C2, C = m_ch.shape
    tm = _tile(T, tm)
    return pl.pallas_call(
        _mm_const_kernel,
        grid=(T // tm, FOURIER_GROUPS),
        in_specs=[pl.BlockSpec((tm, C2), lambda i, g: (i, g)),
                  pl.BlockSpec((C2, C), lambda i, g: (0, 0))],
        out_specs=pl.BlockSpec((tm, C), lambda i, g: (i, g)),
        out_shape=jax.ShapeDtypeStruct((T, FOURIER_GROUPS * C), BF16),
        compiler_params=_params(2),
    )(z, m_ch)


def _band_attn_kernel(q_ref, k_ref, v_ref, o_ref, lse_ref, *, L, tq, win, half, heads):
    hb = pl.program_id(2)
    scale = 1.0 / math.sqrt(HEAD_DIM)
    lane_head = lax.broadcasted_iota(jnp.int32, (tq, LANES), 1) // LSE_REP

    @pl.when(hb == 0)
    def _():
        lse_ref[...] = jnp.zeros(lse_ref.shape, lse_ref.dtype)

    def body(qi, carry):
        q0 = pl.multiple_of(qi * tq, tq)
        ks = pl.multiple_of(jnp.clip(q0 - half, 0, L - win), half)
        qpos = q0 + lax.broadcasted_iota(jnp.int32, (tq, win), 0)
        kpos = ks + lax.broadcasted_iota(jnp.int32, (tq, win), 1)
        band = jnp.abs(qpos - kpos) <= half
        lse_tile = lse_ref[0, 0, pl.ds(q0, tq), :]
        for h in range(heads):
            cs = slice(h * HEAD_DIM, (h + 1) * HEAD_DIM)
            q = q_ref[0, 0, pl.ds(q0, tq), cs]
            k = k_ref[0, 0, pl.ds(ks, win), cs]
            v = v_ref[0, 0, pl.ds(ks, win), cs]
            s = lax.dot_general(q, k, (((1,), (1,)), ((), ())), preferred_element_type=F32) * scale
            s = jnp.where(band, s, NEG_INF)
            m = jnp.max(s, axis=-1, keepdims=True)
            p = jnp.exp(s - m)
            denom = jnp.sum(p, axis=-1, keepdims=True)
            o = jnp.dot(p.astype(BF16), v, preferred_element_type=F32) / denom
            o_ref[0, 0, pl.ds(q0, tq), cs] = o.astype(o_ref.dtype)
            lse_tile = jnp.where(lane_head == hb * heads + h, m + jnp.log(denom), lse_tile)
        lse_ref[0, 0, pl.ds(q0, tq), :] = lse_tile
        return carry

    lax.fori_loop(0, L // tq, body, 0)


def band_attention(qkv, dilation, half, *, B, S, heads_per_step):
    L = S // dilation
    H = HEADS_PER_GROUP
    tq = min(2 * half, L)
    win = min(tq + 2 * half, L)
    hw = heads_per_step * HEAD_DIM
    n_hb = H // heads_per_step

    def spec(which):
        return pl.BlockSpec((1, 1, L, hw), lambda b, r, h: (b, r, 0, which * n_hb + h))

    return pl.pallas_call(
        functools.partial(_band_attn_kernel, L=L, tq=tq, win=win, half=half, heads=heads_per_step),
        grid=(B, dilation, n_hb),
        in_specs=[spec(0), spec(1), spec(2)],
        out_specs=[pl.BlockSpec((1, 1, L, hw), lambda b, r, h: (b, r, 0, h)),
                   pl.BlockSpec((1, 1, L, LANES), lambda b, r, h: (b, r, 0, 0))],
        out_shape=[jax.ShapeDtypeStruct((B, dilation, L, H * HEAD_DIM), BF16),
                   jax.ShapeDtypeStruct((B, dilation, L, LANES), F32)],
        compiler_params=_params(3),
    )(qkv, qkv, qkv)


def _attn_merge_kernel(o0_ref, o1_ref, o2_ref, l0_ref, l1_ref, l2_ref, out_ref):
    l0, l1, l2 = l0_ref[...], l1_ref[...], l2_ref[...]
    m = jnp.maximum(jnp.maximum(l0, l1), l2)
    e0, e1, e2 = jnp.exp(l0 - m), jnp.exp(l1 - m), jnp.exp(l2 - m)
    inv = 1.0 / (e0 + e1 + e2)
    w0, w1, w2 = e0 * inv, e1 * inv, e2 * inv
    for h in range(HEADS_PER_GROUP):
        cs = slice(h * HEAD_DIM, (h + 1) * HEAD_DIM)
        c = h * LSE_REP
        acc = (w0[:, c:c + 1] * o0_ref[:, cs].astype(F32) + w1[:, c:c + 1] * o1_ref[:, cs].astype(F32)
               + w2[:, c:c + 1] * o2_ref[:, cs].astype(F32))
        out_ref[:, cs] = acc.astype(out_ref.dtype)


def attn_merge(outs, lses, *, tm):
    T, W = outs[0].shape
    tm = _tile(T, tm)
    o_spec = pl.BlockSpec((tm, W), lambda i: (i, 0))
    l_spec = pl.BlockSpec((tm, LANES), lambda i: (i, 0))
    return pl.pallas_call(
        _attn_merge_kernel, grid=(T // tm,),
        in_specs=[o_spec] * 3 + [l_spec] * 3, out_specs=o_spec,
        out_shape=jax.ShapeDtypeStruct((T, W), BF16),
        compiler_params=_params(1),
    )(*outs, *lses)


def _ffn_up_kernel(x_ref, wg_ref, wu_ref, o_ref):
    x = x_ref[0]
    g = jnp.dot(x, wg_ref[0, 0].astype(BF16), preferred_element_type=F32)
    u = jnp.dot(x, wu_ref[0, 0].astype(BF16), preferred_element_type=F32)
    o_ref[0] = (g * jax.nn.sigmoid(g) * u).astype(o_ref.dtype)


def _ffn_down_kernel(h_ref, wd_ref, gate_ref, o_ref):
    acc = jnp.dot(h_ref[0], wd_ref[0, 0].astype(BF16), preferred_element_type=F32)
    o_ref[0] = (acc * gate_ref[0]).astype(o_ref.dtype)


def expert_ffn(xe, w_gate, w_up, w_down, layer, gates, *, tm, tn):
    E, C, D = xe.shape
    F = w_gate.shape[3]
    tm = _tile(C, tm)
    tf = _tile(F, tn)
    td = _tile(D, tn)
    hid = pl.pallas_call(
        _ffn_up_kernel,
        grid=(E, C // tm, F // tf),
        in_specs=[pl.BlockSpec((1, tm, D), lambda e, i, f: (e, i, 0)),
                  pl.BlockSpec((1, 1, D, tf), lambda e, i, f: (layer, e, 0, f)),
                  pl.BlockSpec((1, 1, D, tf), lambda e, i, f: (layer, e, 0, f))],
        out_specs=pl.BlockSpec((1, tm, tf), lambda e, i, f: (e, i, f)),
        out_shape=jax.ShapeDtypeStruct((E, C, F), BF16),
        compiler_params=_params(3),
    )(xe, w_gate, w_up)
    return pl.pallas_call(
        _ffn_down_kernel,
        grid=(E, C // tm, D // td),
        in_specs=[pl.BlockSpec((1, tm, F), lambda e, i, j: (e, i, 0)),
                  pl.BlockSpec((1, 1, F, td), lambda e, i, j: (layer, e, 0, j)),
                  pl.BlockSpec((1, tm, 1), lambda e, i, j: (e, i, 0))],
        out_specs=pl.BlockSpec((1, tm, td), lambda e, i, j: (e, i, j)),
        out_shape=jax.ShapeDtypeStruct((E, C, D), BF16),
        compiler_params=_params(3),
    )(hid, w_down, gates)


COMBINE_TOKENS = 256
SLOT_BLOCK = 64
SLOT_WINDOW = 2 * SLOT_BLOCK


def _combine_kernel(base_ref, nr_ref, x_ref, gt_ref, pos_ref, *refs, n_exp):
    y_refs, o_ref, acc_ref = refs[:2 * n_exp], refs[2 * n_exp], refs[2 * n_exp + 1]
    tb, r = pl.program_id(0), pl.program_id(1)
    n_tok = x_ref.shape[0]

    @pl.when(r == 0)
    def _():
        acc_ref[...] = jnp.zeros(acc_ref.shape, acc_ref.dtype)

    @pl.when(r < nr_ref[tb])
    def _():
        lane = lax.broadcasted_iota(jnp.int32, (n_tok, SLOT_WINDOW), 1)
        pos = pos_ref[...]
        onehots = []
        for e in range(n_exp):
            start = base_ref[tb * n_exp + e] + r * SLOT_WINDOW
            onehots.append(jnp.where(pos[:, e:e + 1] - start == lane, 1.0, 0.0).astype(BF16))
        onehot = jnp.concatenate(onehots, axis=1)
        rows = jnp.concatenate([y[0] for y in y_refs], axis=0)
        acc_ref[...] += jnp.dot(onehot, rows, preferred_element_type=F32)

    @pl.when(r == pl.num_programs(1) - 1)
    def _():
        o_ref[...] = x_ref[...] + gt_ref[0] * acc_ref[...]


def moe_combine(x, gate, ye, pos_tm, base, n_rounds, *, rows_per_seq):
    T, D = x.shape
    E, C, _ = ye.shape
    tb = COMBINE_TOKENS
    per_seq = rows_per_seq // tb
    max_rounds = -(-(SLOT_BLOCK - 1 + tb) // SLOT_WINDOW)
    last_blk = C // SLOT_BLOCK - 1

    def y_spec(e, k):
        def imap(t, r, base_ref, nr_ref):
            rr = jnp.minimum(r, nr_ref[t] - 1)
            blk = base_ref[t * E + e] // SLOT_BLOCK + 2 * rr + k
            return (e, jnp.minimum(blk, last_blk), 0)
        return pl.BlockSpec((1, SLOT_BLOCK, D), imap)

    grid_spec = pltpu.PrefetchScalarGridSpec(
        num_scalar_prefetch=2,
        grid=(T // tb, max_rounds),
        in_specs=[pl.BlockSpec((tb, D), lambda t, r, b, n: (t, 0)),
                  pl.BlockSpec((1, 1, D), lambda t, r, b, n: (t // per_seq, 0, 0)),
                  pl.BlockSpec((tb, LANES), lambda t, r, b, n: (t, 0))]
                 + [y_spec(e, k) for e in range(E) for k in range(2)],
        out_specs=pl.BlockSpec((tb, D), lambda t, r, b, n: (t, 0)),
        scratch_shapes=[pltpu.VMEM((tb, D), F32)],
    )
    return pl.pallas_call(
        functools.partial(_combine_kernel, n_exp=E),
        grid_spec=grid_spec,
        out_shape=jax.ShapeDtypeStruct((T, D), F32),
        compiler_params=_params(2),
    )(base, n_rounds, x, gate, pos_tm, *([ye] * (2 * E)))


def _route(logits, cap):
    aff_t = jax.nn.softmax(logits[:, :N_EXPERTS], axis=-1).T
    thr = lax.top_k(aff_t, cap)[0][:, -1:]
    above = aff_t > thr
    tied = aff_t == thr
    need = cap - jnp.sum(above, axis=1, keepdims=True)
    sel = above | (tied & (jnp.cumsum(tied, axis=1) - 1 < need))
    csum = jnp.cumsum(sel, axis=1, dtype=jnp.int32)
    pos = jnp.where(sel, csum - 1, -1)
    slots = jnp.arange(1, cap + 1, dtype=jnp.int32)
    idx = jax.vmap(lambda c: jnp.searchsorted(c, slots, side="left"))(csum).astype(jnp.int32)
    gates = jnp.take_along_axis(aff_t, idx, axis=1)
    return idx, gates, pos, csum


def _moe(x, gate, h, logits, w_gate, w_up, w_down, layer, *, rows_per_seq):
    T, D = h.shape
    cap = EC_CAPACITY_FACTOR * T // N_EXPERTS
    idx, gates, pos, csum = _route(logits, cap)
    xe = h[idx]
    ye = expert_ffn(xe, w_gate, w_up, w_down, layer, gates[..., None], tm=1024, tn=512)
    n_tb = T // COMBINE_TOKENS
    ends = csum[:, COMBINE_TOKENS - 1::COMBINE_TOKENS].T
    starts = jnp.concatenate([jnp.zeros((1, N_EXPERTS), jnp.int32), ends[:-1]], axis=0)
    base = (starts // SLOT_BLOCK) * SLOT_BLOCK
    n_rounds = jnp.maximum(1, jnp.max(-(-(ends - base) // SLOT_WINDOW), axis=1)).astype(jnp.int32)
    pos_tm = jnp.pad(pos.T, ((0, 0), (0, LANES - N_EXPERTS)), constant_values=-1)
    return moe_combine(x, gate, ye, pos_tm, base.reshape(n_tb * N_EXPERTS), n_rounds,
                       rows_per_seq=rows_per_seq)


def _ada_modulation(c_all, w_ada, b_ada):
    n = c_all.shape[0]
    pad = (-n) % 16
    a = jnp.pad(jax.nn.silu(c_all), ((0, pad), (0, 0))).astype(BF16)
    return matmul_bias_layers(a, w_ada, b_ada[:, None, :], tn=1024)[:, :n]


def _to_classes(a, B, S, dilation):
    W = a.shape[1]
    if dilation == 1:
        return a.reshape(B, 1, S, W)
    return a.reshape(B, S // dilation, dilation, W).transpose(0, 2, 1, 3)


def _from_classes(a, B, S):
    _, dilation, L, W = a.shape
    if dilation == 1:
        return a.reshape(B * S, W)
    return a.transpose(0, 2, 1, 3).reshape(B * S, W)


def _run_trunk(x, mods, g_mix, g_ffn, w_fourier_out, w_qkv, w_attn_out, w_router_pad,
               w_gate, w_up, w_down, g_final, tables):
    B, S, D = x.shape
    T = B * S
    m_seq, m_ch, rope = tables
    x = x.reshape(T, D)
    depth = g_mix.shape[0]
    for i in range(depth):
        sh1, sc1, gt1, sh2, sc2, gt2 = [t[:, None, :] for t in jnp.split(mods[i], 6, axis=-1)]
        h = norm_modulate(x, g_mix[i][None, :], sh1, sc1, rows_per_seq=S, ts=512)
        if i % N_MIXERS == 0:
            z = dft_seq(m_seq, h.reshape(B, S, D), tm=1024)
            mixed = dft_channel(z.reshape(T, 2 * D), m_ch, tm=1024)
            x = matmul_residual(mixed, w_fourier_out, i // N_MIXERS, x, gt1, rows_per_seq=S, tm=1024, tn=512)
        else:
            outs, lses = [], []
            for g, (window, dilation) in enumerate(DILATION_GROUPS):
                qkv = matmul_qkv_rope(h, w_qkv, i // N_MIXERS, g, *rope, rows_per_seq=S, tm=1024, tn=512)
                o, lse = band_attention(_to_classes(qkv, B, S, dilation), dilation, window // (2 * dilation),
                                        B=B, S=S, heads_per_step=4)
                outs.append(_from_classes(o, B, S))
                lses.append(_from_classes(lse, B, S))
            merged = attn_merge(outs, lses, tm=512)
            x = matmul_residual(merged, w_attn_out, i // N_MIXERS, x, gt1, rows_per_seq=S, tm=1024, tn=512)
        h, logits = norm_modulate(x, g_ffn[i][None, :], sh2, sc2, rows_per_seq=S, ts=512,
                                  w_router=w_router_pad[i])
        x = _moe(x, gt2, h, logits, w_gate, w_up, w_down, i, rows_per_seq=S)
    return final_norm(x, g_final[None, :], ts=512).reshape(B, S, D)


def kernel(x_prompt, x_sample, c_prompt, c_sample, g_mix, g_ffn, w_ada, b_ada, w_fourier_out, w_qkv,
           w_attn_out, w_router, w_gate, w_up, w_down, g_final):
    nb = x_prompt.shape[0]
    mods = _ada_modulation(jnp.concatenate([c_prompt, c_sample], axis=0), w_ada, b_ada)
    w_router_pad = jnp.pad(w_router, ((0, 0), (0, 0), (0, LANES - w_router.shape[2]))).astype(BF16)
    w_qkv = permute_qk_columns(w_qkv)
    outs = []
    table_cache = {}
    for x, lo, hi in ((x_prompt, 0, nb), (x_sample, nb, nb + x_sample.shape[0])):
        S, D = x.shape[1], x.shape[2]
        if S not in table_cache:
            table_cache[S] = dft_tables(S, D // FOURIER_GROUPS) + (rope_tables(S),)
        tables = table_cache[S]
        outs.append(_run_trunk(x, mods[:, lo:hi], g_mix, g_ffn, w_fourier_out, w_qkv,
                               w_attn_out, w_router_pad, w_gate, w_up, w_down, g_final, tables))
    return tuple(outs)
```

```python
import functools
import math

import jax
import jax.numpy as jnp
from jax import lax
from jax.experimental import pallas as pl
from jax.experimental.pallas import tpu as pltpu

BF16 = jnp.bfloat16
F32 = jnp.float32

N_MIXERS = 2
FOURIER_GROUPS = 4
HEAD_DIM = 128
HEADS_PER_GROUP = 8
DILATION_GROUPS = ((128, 1), (512, 4), (2048, 16))
N_ATTN_GROUPS = 3
ROT_DIM = HEAD_DIM // 4
ROPE_THETA = 500000.0
N_EXPERTS = 16
EC_CAPACITY_FACTOR = 2
NORM_EPS = 1e-6
NEG_INF = -1e30

V7X_VMEM_BYTES = 64 * 1024 * 1024
VMEM_LIMIT = V7X_VMEM_BYTES - 8 * 1024 * 1024
LANES = 128
LSE_REP = LANES // HEADS_PER_GROUP


def _params(n_axes):
    return pltpu.CompilerParams(dimension_semantics=("arbitrary",) * n_axes,
                                vmem_limit_bytes=VMEM_LIMIT)


def _tile(n, want):
    t = min(n, want)
    assert n % t == 0, (n, want)
    return t


def _mm_bias_kernel(a_ref, w_ref, b_ref, o_ref):
    acc = jnp.dot(a_ref[...], w_ref[0].astype(BF16), preferred_element_type=F32)
    o_ref[0] = (acc + b_ref[0]).astype(o_ref.dtype)


def _mm_residual_kernel(a_ref, w_ref, x_ref, g_ref, o_ref):
    acc = jnp.dot(a_ref[...], w_ref[0].astype(BF16), preferred_element_type=F32)
    o_ref[...] = x_ref[...] + g_ref[0] * acc


def matmul_bias_layers(a, w, b, *, tn):
    M, K = a.shape
    n_layers, _, N = w.shape
    tn = _tile(N, tn)
    return pl.pallas_call(
        _mm_bias_kernel,
        grid=(n_layers, N // tn),
        in_specs=[pl.BlockSpec((M, K), lambda l, j: (0, 0)),
                  pl.BlockSpec((1, K, tn), lambda l, j: (l, 0, j)),
                  pl.BlockSpec((1, 1, tn), lambda l, j: (l, 0, j))],
        out_specs=pl.BlockSpec((1, M, tn), lambda l, j: (l, 0, j)),
        out_shape=jax.ShapeDtypeStruct((n_layers, M, N), F32),
        compiler_params=_params(2),
    )(a, w, b)


def matmul_residual(a, w, layer, x, gate, *, rows_per_seq, tm, tn):
    T, K = a.shape
    N = w.shape[2]
    tm = _tile(rows_per_seq, tm)
    tn = _tile(N, tn)
    per_seq = rows_per_seq // tm
    return pl.pallas_call(
        _mm_residual_kernel,
        grid=(T // tm, N // tn),
        in_specs=[pl.BlockSpec((tm, K), lambda i, j: (i, 0)),
                  pl.BlockSpec((1, K, tn), lambda i, j: (layer, 0, j)),
                  pl.BlockSpec((tm, tn), lambda i, j: (i, j)),
                  pl.BlockSpec((1, 1, tn), lambda i, j: (i // per_seq, 0, j))],
        out_specs=pl.BlockSpec((tm, tn), lambda i, j: (i, j)),
        out_shape=jax.ShapeDtypeStruct((T, N), F32),
        compiler_params=_params(2),
    )(a, w, x, gate)


def permute_qk_columns(w_qkv):
    n_layers, D, N = w_qkv.shape
    w = w_qkv.reshape(n_layers, D, 3, N // (3 * HEAD_DIM), HEAD_DIM)
    half, mid = ROT_DIM // 2, HEAD_DIM // 2
    qk = w[:, :, :2]
    qk = jnp.concatenate([qk[..., :half], qk[..., ROT_DIM:ROT_DIM + mid - half],
                          qk[..., half:ROT_DIM], qk[..., ROT_DIM + mid - half:]], axis=-1)
    return jnp.concatenate([qk, w[:, :, 2:]], axis=2).reshape(n_layers, D, N).astype(BF16)


def rope_tables(S):
    half, mid = ROT_DIM // 2, HEAD_DIM // 2
    inv_freq = ROPE_THETA ** (-jnp.arange(half, dtype=F32) / half)
    ang = jnp.arange(S, dtype=F32)[:, None] * inv_freq[None, :]
    cos, sin = jnp.cos(ang), jnp.sin(ang)
    one = jnp.ones((S, mid - half), F32)
    zero = jnp.zeros((S, mid - half), F32)
    cos_t = jnp.concatenate([cos, one, cos, one], axis=1)
    sin_t = jnp.concatenate([-sin, zero, sin, zero], axis=1)
    cos_all = jnp.stack([cos_t, jnp.ones_like(cos_t)])
    sin_all = jnp.stack([sin_t, jnp.zeros_like(sin_t)])
    return cos_all, sin_all


def _qkv_rope_kernel(a_ref, w_ref, cos_ref, sin_ref, o_ref):
    acc = jnp.dot(a_ref[...], w_ref[0].astype(BF16), preferred_element_type=F32)
    cos, sin = cos_ref[0], sin_ref[0]
    for h in range(acc.shape[1] // HEAD_DIM):
        xh = acc[:, h * HEAD_DIM:(h + 1) * HEAD_DIM]
        rot = pltpu.roll(xh, HEAD_DIM // 2, 1)
        o_ref[:, h * HEAD_DIM:(h + 1) * HEAD_DIM] = (xh * cos + rot * sin).astype(o_ref.dtype)


def matmul_qkv_rope(h, w_qkv, layer, group, cos_all, sin_all, *, rows_per_seq, tm, tn):
    T, K = h.shape
    gw = HEADS_PER_GROUP * HEAD_DIM
    tm = _tile(rows_per_seq, tm)
    tn = _tile(gw, tn)
    per_seq = rows_per_seq // tm
    n_per = gw // tn
    tab = pl.BlockSpec((1, tm, HEAD_DIM), lambda i, j: (j // (2 * n_per), i % per_seq, 0))
    return pl.pallas_call(
        _qkv_rope_kernel,
        grid=(T // tm, 3 * n_per),
        in_specs=[pl.BlockSpec((tm, K), lambda i, j: (i, 0)),
                  pl.BlockSpec((1, K, tn), lambda i, j: (layer, 0, (j // n_per) * N_ATTN_GROUPS * n_per
                                                         + group * n_per + j % n_per)),
                  tab, tab],
        out_specs=pl.BlockSpec((tm, tn), lambda i, j: (i, j)),
        out_shape=jax.ShapeDtypeStruct((T, 3 * gw), BF16),
        compiler_params=_params(2),
    )(h, w_qkv, cos_all, sin_all)


def _norm_mod_kernel(x_ref, g_ref, sh_ref, sc_ref, o_ref):
    x = x_ref[...]
    y = x * lax.rsqrt(jnp.mean(x * x, axis=-1, keepdims=True) + NORM_EPS) * g_ref[...]
    o_ref[...] = (y * (1.0 + sc_ref[0]) + sh_ref[0]).astype(o_ref.dtype)


def _norm_mod_router_kernel(x_ref, g_ref, sh_ref, sc_ref, wr_ref, o_ref, lg_ref):
    x = x_ref[...]
    y = x * lax.rsqrt(jnp.mean(x * x, axis=-1, keepdims=True) + NORM_EPS) * g_ref[...]
    h = (y * (1.0 + sc_ref[0]) + sh_ref[0]).astype(BF16)
    o_ref[...] = h
    lg_ref[...] = jnp.dot(h, wr_ref[...], preferred_element_type=F32)


def norm_modulate(x, g, shift, scale, *, rows_per_seq, ts, w_router=None):
    T, D = x.shape
    ts = _tile(rows_per_seq, ts)
    per_seq = rows_per_seq // ts
    row = pl.BlockSpec((ts, D), lambda i: (i, 0))
    vec = pl.BlockSpec((1, D), lambda i: (0, 0))
    seq = pl.BlockSpec((1, 1, D), lambda i: (i // per_seq, 0, 0))
    if w_router is None:
        return pl.pallas_call(
            _norm_mod_kernel, grid=(T // ts,),
            in_specs=[row, vec, seq, seq], out_specs=row,
            out_shape=jax.ShapeDtypeStruct((T, D), BF16),
            compiler_params=_params(1),
        )(x, g, shift, scale)
    return pl.pallas_call(
        _norm_mod_router_kernel, grid=(T // ts,),
        in_specs=[row, vec, seq, seq, pl.BlockSpec((D, LANES), lambda i: (0, 0))],
        out_specs=[row, pl.BlockSpec((ts, LANES), lambda i: (i, 0))],
        out_shape=[jax.ShapeDtypeStruct((T, D), BF16), jax.ShapeDtypeStruct((T, LANES), F32)],
        compiler_params=_params(1),
    )(x, g, shift, scale, w_router)


def _final_norm_kernel(x_ref, g_ref, o_ref):
    x = x_ref[...]
    o_ref[...] = x * lax.rsqrt(jnp.mean(x * x, axis=-1, keepdims=True) + NORM_EPS) * g_ref[...]


def final_norm(x, g, *, ts):
    T, D = x.shape
    ts = _tile(T, ts)
    return pl.pallas_call(
        _final_norm_kernel, grid=(T // ts,),
        in_specs=[pl.BlockSpec((ts, D), lambda i: (i, 0)), pl.BlockSpec((1, D), lambda i: (0, 0))],
        out_specs=pl.BlockSpec((ts, D), lambda i: (i, 0)),
        out_shape=jax.ShapeDtypeStruct((T, D), F32),
        compiler_params=_params(1),
    )(x, g)


def dft_tables(S, C):
    r = int(round(math.sqrt(S)))
    assert r * r == S
    k = jnp.arange(S, dtype=jnp.int32)[:, None]
    n = jnp.arange(r, dtype=jnp.int32)[None, :]
    ang_hi = ((k * n * r) % S).astype(F32) * (2.0 * math.pi / S)
    ang_lo = ((k * n) % S).astype(F32) * (2.0 * math.pi / S)
    ch, sh_ = jnp.cos(ang_hi)[:, :, None], jnp.sin(ang_hi)[:, :, None]
    cl, sl = jnp.cos(ang_lo)[:, None, :], jnp.sin(ang_lo)[:, None, :]
    scale = float(S) ** -0.5
    cos_s = ((ch * cl - sh_ * sl) * scale).reshape(S, S)
    sin_s = ((sh_ * cl + ch * sl) * scale).reshape(S, S)
    m_seq = jnp.concatenate([cos_s, sin_s], axis=0).astype(BF16)
    kc = jnp.arange(C, dtype=jnp.int32)
    ang_c = ((kc[:, None] * kc[None, :]) % C).astype(F32) * (2.0 * math.pi / C)
    cscale = float(C) ** -0.5
    m_ch = jnp.concatenate([jnp.cos(ang_c) * cscale, -jnp.sin(ang_c) * cscale], axis=0).astype(BF16)
    return m_seq, m_ch


def _dft_seq_kernel(m_ref, h_ref, o_ref):
    o_ref[0] = jnp.dot(m_ref[...], h_ref[0], preferred_element_type=F32).astype(o_ref.dtype)


def dft_seq(m_seq, h3, *, tm):
    B, S, D = h3.shape
    C = D // FOURIER_GROUPS
    tm = _tile(S, tm)
    n_i = S // tm
    return pl.pallas_call(
        _dft_seq_kernel,
        grid=(2 * n_i, B, FOURIER_GROUPS),
        in_specs=[pl.BlockSpec((tm, S), lambda i, b, g: (i, 0)),
                  pl.BlockSpec((1, S, C), lambda i, b, g: (b, 0, g))],
        out_specs=pl.BlockSpec((1, tm, C), lambda i, b, g: (b, i % n_i, 2 * g + i // n_i)),
        out_shape=jax.ShapeDtypeStruct((B, S, 2 * D), BF16),
        compiler_params=_params(3),
    )(m_seq, h3)


def _mm_const_kernel(a_ref, w_ref, o_ref):
    o_ref[...] = jnp.dot(a_ref[...], w_ref[...], preferred_element_type=F32).astype(o_ref.dtype)


def dft_channel(z, m_ch, *, tm):
    T = z.shape[0]
    C2, C = m_ch.shape
    tm = _tile(T, tm)
    return pl.pallas_call(
        _mm_const_kernel,
        grid=(T // tm, FOURIER_GROUPS),
        in_specs=[pl.BlockSpec((tm, C2), lambda i, g: (i, g)),
                  pl.BlockSpec((C2, C), lambda i, g: (0, 0))],
        out_specs=pl.BlockSpec((tm, C), lambda i, g: (i, g)),
        out_shape=jax.ShapeDtypeStruct((T, FOURIER_GROUPS * C), BF16),
        compiler_params=_params(2),
    )(z, m_ch)


def _band_attn_kernel(q_ref, k_ref, v_ref, o_ref, lse_ref, *, L, tq, win, half, heads):
    hb = pl.program_id(2)
    scale = 1.0 / math.sqrt(HEAD_DIM)
    lane_head = lax.broadcasted_iota(jnp.int32, (tq, LANES), 1) // LSE_REP

    @pl.when(hb == 0)
    def _():
        lse_ref[...] = jnp.zeros(lse_ref.shape, lse_ref.dtype)

    def body(qi, carry):
        q0 = pl.multiple_of(qi * tq, tq)
        ks = pl.multiple_of(jnp.clip(q0 - half, 0, L - win), half)
        qpos = q0 + lax.broadcasted_iota(jnp.int32, (tq, win), 0)
        kpos = ks + lax.broadcasted_iota(jnp.int32, (tq, win), 1)
        band = jnp.abs(qpos - kpos) <= half
        lse_tile = lse_ref[0, 0, pl.ds(q0, tq), :]
        for h in range(heads):
            cs = slice(h * HEAD_DIM, (h + 1) * HEAD_DIM)
            q = q_ref[0, 0, pl.ds(q0, tq), cs]
            k = k_ref[0, 0, pl.ds(ks, win), cs]
            v = v_ref[0, 0, pl.ds(ks, win), cs]
            s = lax.dot_general(q, k, (((1,), (1,)), ((), ())), preferred_element_type=F32) * scale
            s = jnp.where(band, s, NEG_INF)
            m = jnp.max(s, axis=-1, keepdims=True)
            p = jnp.exp(s - m)
            denom = jnp.sum(p, axis=-1, keepdims=True)
            o = jnp.dot(p.astype(BF16), v, preferred_element_type=F32) / denom
            o_ref[0, 0, pl.ds(q0, tq), cs] = o.astype(o_ref.dtype)
            lse_tile = jnp.where(lane_head == hb * heads + h, m + jnp.log(denom), lse_tile)
        lse_ref[0, 0, pl.ds(q0, tq), :] = lse_tile
        return carry

    lax.fori_loop(0, L // tq, body, 0)


ATTN_BLOCK_BYTES = 4 * 1024 * 1024


def band_attention(qkv, dilation, half, *, B, S):
    L = S // dilation
    H = HEADS_PER_GROUP
    tq = min(2 * half, L)
    win = min(tq + 2 * half, L)
    heads_per_step = max(n for n in (1, 2, 4, 8) if H % n == 0 and (n == 1 or L * n * HEAD_DIM * 2 <= ATTN_BLOCK_BYTES))
    hw = heads_per_step * HEAD_DIM
    n_hb = H // heads_per_step

    def spec(which):
        return pl.BlockSpec((1, 1, L, hw), lambda b, r, h: (b, r, 0, which * n_hb + h))

    return pl.pallas_call(
        functools.partial(_band_attn_kernel, L=L, tq=tq, win=win, half=half, heads=heads_per_step),
        grid=(B, dilation, n_hb),
        in_specs=[spec(0), spec(1), spec(2)],
        out_specs=[pl.BlockSpec((1, 1, L, hw), lambda b, r, h: (b, r, 0, h)),
                   pl.BlockSpec((1, 1, L, LANES), lambda b, r, h: (b, r, 0, 0))],
        out_shape=[jax.ShapeDtypeStruct((B, dilation, L, H * HEAD_DIM), BF16),
                   jax.ShapeDtypeStruct((B, dilation, L, LANES), F32)],
        compiler_params=_params(3),
    )(qkv, qkv, qkv)


def _attn_merge_kernel(o0_ref, o1_ref, o2_ref, l0_ref, l1_ref, l2_ref, out_ref):
    l0, l1, l2 = l0_ref[...], l1_ref[...], l2_ref[...]
    m = jnp.maximum(jnp.maximum(l0, l1), l2)
    e0, e1, e2 = jnp.exp(l0 - m), jnp.exp(l1 - m), jnp.exp(l2 - m)
    inv = 1.0 / (e0 + e1 + e2)
    w0, w1, w2 = e0 * inv, e1 * inv, e2 * inv
    for h in range(HEADS_PER_GROUP):
        cs = slice(h * HEAD_DIM, (h + 1) * HEAD_DIM)
        c = h * LSE_REP
        acc = (w0[:, c:c + 1] * o0_ref[:, cs].astype(F32) + w1[:, c:c + 1] * o1_ref[:, cs].astype(F32)
               + w2[:, c:c + 1] * o2_ref[:, cs].astype(F32))
        out_ref[:, cs] = acc.astype(out_ref.dtype)


def attn_merge(outs, lses, *, tm):
    T, W = outs[0].shape
    tm = _tile(T, tm)
    o_spec = pl.BlockSpec((tm, W), lambda i: (i, 0))
    l_spec = pl.BlockSpec((tm, LANES), lambda i: (i, 0))
    return pl.pallas_call(
        _attn_merge_kernel, grid=(T // tm,),
        in_specs=[o_spec] * 3 + [l_spec] * 3, out_specs=o_spec,
        out_shape=jax.ShapeDtypeStruct((T, W), BF16),
        compiler_params=_params(1),
    )(*outs, *lses)


def _ffn_up_kernel(x_ref, wg_ref, wu_ref, o_ref):
    x = x_ref[0]
    g = jnp.dot(x, wg_ref[0, 0].astype(BF16), preferred_element_type=F32)
    u = jnp.dot(x, wu_ref[0, 0].astype(BF16), preferred_element_type=F32)
    o_ref[0] = (g * jax.nn.sigmoid(g) * u).astype(o_ref.dtype)


def _ffn_down_kernel(h_ref, wd_ref, gate_ref, o_ref):
    acc = jnp.dot(h_ref[0], wd_ref[0, 0].astype(BF16), preferred_element_type=F32)
    o_ref[0] = (acc * gate_ref[0]).astype(o_ref.dtype)


def expert_ffn(xe, w_gate, w_up, w_down, layer, gates, *, tm, tf, td):
    E, C, D = xe.shape
    F = w_gate.shape[3]
    tm = _tile(C, tm)
    tf = _tile(F, tf)
    td = _tile(D, td)
    hid = pl.pallas_call(
        _ffn_up_kernel,
        grid=(E, C // tm, F // tf),
        in_specs=[pl.BlockSpec((1, tm, D), lambda e, i, f: (e, i, 0)),
                  pl.BlockSpec((1, 1, D, tf), lambda e, i, f: (layer, e, 0, f)),
                  pl.BlockSpec((1, 1, D, tf), lambda e, i, f: (layer, e, 0, f))],
        out_specs=pl.BlockSpec((1, tm, tf), lambda e, i, f: (e, i, f)),
        out_shape=jax.ShapeDtypeStruct((E, C, F), BF16),
        compiler_params=_params(3),
    )(xe, w_gate, w_up)
    return pl.pallas_call(
        _ffn_down_kernel,
        grid=(E, C // tm, D // td),
        in_specs=[pl.BlockSpec((1, tm, F), lambda e, i, j: (e, i, 0)),
                  pl.BlockSpec((1, 1, F, td), lambda e, i, j: (layer, e, 0, j)),
                  pl.BlockSpec((1, tm, 1), lambda e, i, j: (e, i, 0))],
        out_specs=pl.BlockSpec((1, tm, td), lambda e, i, j: (e, i, j)),
        out_shape=jax.ShapeDtypeStruct((E, C, D), BF16),
        compiler_params=_params(3),
    )(hid, w_down, gates)


COMBINE_TOKENS = 256
SLOT_BLOCK = 64
SLOT_WINDOW = 2 * SLOT_BLOCK


def _combine_kernel(base_ref, nr_ref, x_ref, gt_ref, pos_ref, *refs, n_exp):
    y_refs, o_ref, acc_ref = refs[:2 * n_exp], refs[2 * n_exp], refs[2 * n_exp + 1]
    tb, r = pl.program_id(0), pl.program_id(1)
    n_tok = x_ref.shape[0]

    @pl.when(r == 0)
    def _():
        acc_ref[...] = jnp.zeros(acc_ref.shape, acc_ref.dtype)

    @pl.when(r < nr_ref[tb])
    def _():
        lane = lax.broadcasted_iota(jnp.int32, (n_tok, SLOT_WINDOW), 1)
        pos = pos_ref[...]
        onehots = []
        for e in range(n_exp):
            start = base_ref[tb * n_exp + e] + r * SLOT_WINDOW
            onehots.append(jnp.where(pos[:, e:e + 1] - start == lane, 1.0, 0.0).astype(BF16))
        onehot = jnp.concatenate(onehots, axis=1)
        rows = jnp.concatenate([y[0] for y in y_refs], axis=0)
        acc_ref[...] += jnp.dot(onehot, rows, preferred_element_type=F32)

    @pl.when(r == pl.num_programs(1) - 1)
    def _():
        o_ref[...] = x_ref[...] + gt_ref[0] * acc_ref[...]


def moe_combine(x, gate, ye, pos_tm, base, n_rounds, *, rows_per_seq):
    T, D = x.shape
    E, C, _ = ye.shape
    tb = COMBINE_TOKENS
    per_seq = rows_per_seq // tb
    max_rounds = -(-(SLOT_BLOCK - 1 + tb) // SLOT_WINDOW)
    last_blk = C // SLOT_BLOCK - 1

    def y_spec(e, k):
        def imap(t, r, base_ref, nr_ref):
            rr = jnp.minimum(r, nr_ref[t] - 1)
            blk = base_ref[t * E + e] // SLOT_BLOCK + 2 * rr + k
            return (e, jnp.minimum(blk, last_blk), 0)
        return pl.BlockSpec((1, SLOT_BLOCK, D), imap)

    grid_spec = pltpu.PrefetchScalarGridSpec(
        num_scalar_prefetch=2,
        grid=(T // tb, max_rounds),
        in_specs=[pl.BlockSpec((tb, D), lambda t, r, b, n: (t, 0)),
                  pl.BlockSpec((1, 1, D), lambda t, r, b, n: (t // per_seq, 0, 0)),
                  pl.BlockSpec((tb, LANES), lambda t, r, b, n: (t, 0))]
                 + [y_spec(e, k) for e in range(E) for k in range(2)],
        out_specs=pl.BlockSpec((tb, D), lambda t, r, b, n: (t, 0)),
        scratch_shapes=[pltpu.VMEM((tb, D), F32)],
    )
    return pl.pallas_call(
        functools.partial(_combine_kernel, n_exp=E),
        grid_spec=grid_spec,
        out_shape=jax.ShapeDtypeStruct((T, D), F32),
        compiler_params=_params(2),
    )(base, n_rounds, x, gate, pos_tm, *([ye] * (2 * E)))


ROUTE_BLOCK = 512


def _route_threshold_kernel(aff_ref, thr_ref, need_ref, *, cap):
    bits = pltpu.bitcast(aff_ref[...], jnp.int32)

    def step(i, v):
        cand = v | (1 << (30 - i))
        cnt = jnp.sum(jnp.where(bits >= cand, 1, 0), axis=1, keepdims=True)
        return jnp.where(cnt >= cap, cand, v)

    v = lax.fori_loop(0, 31, step, jnp.zeros((bits.shape[0], 1), jnp.int32))
    thr_ref[...] = v
    need_ref[...] = cap - jnp.sum(jnp.where(bits > v, 1, 0), axis=1, keepdims=True)


def _route_scan_kernel(aff_ref, thr_ref, need_ref, tri_ref, pos_ref, csum_ref, tied_acc, sel_acc):
    @pl.when(pl.program_id(0) == 0)
    def _():
        tied_acc[...] = jnp.zeros(tied_acc.shape, tied_acc.dtype)
        sel_acc[...] = jnp.zeros(sel_acc.shape, sel_acc.dtype)

    bits = pltpu.bitcast(aff_ref[...], jnp.int32)
    thr = thr_ref[...]
    tri = tri_ref[...]
    tied = bits == thr
    tied_count = tied_acc[...] + jnp.dot(jnp.where(tied, 1.0, 0.0).astype(BF16), tri, preferred_element_type=F32)
    sel = (bits > thr) | (tied & (tied_count.astype(jnp.int32) - 1 < need_ref[...]))
    sel_count = sel_acc[...] + jnp.dot(jnp.where(sel, 1.0, 0.0).astype(BF16), tri, preferred_element_type=F32)
    csum = sel_count.astype(jnp.int32)
    pos_ref[...] = jnp.where(sel, csum - 1, -1)
    csum_ref[...] = csum
    tied_acc[...] = tied_count[:, -1:]
    sel_acc[...] = sel_count[:, -1:]


def _route(logits, cap):
    aff_t = jax.nn.softmax(logits[:, :N_EXPERTS], axis=-1).T
    E, T = aff_t.shape
    thr, need = pl.pallas_call(
        functools.partial(_route_threshold_kernel, cap=cap),
        out_shape=[jax.ShapeDtypeStruct((E, 1), jnp.int32)] * 2,
        compiler_params=pltpu.CompilerParams(vmem_limit_bytes=VMEM_LIMIT),
    )(aff_t)
    rb = _tile(T, ROUTE_BLOCK)
    tri = (jnp.arange(rb)[:, None] <= jnp.arange(rb)[None, :]).astype(BF16)
    col = pl.BlockSpec((E, 1), lambda i: (0, 0))
    blk = pl.BlockSpec((E, rb), lambda i: (0, i))
    pos, csum = pl.pallas_call(
        _route_scan_kernel,
        grid=(T // rb,),
        in_specs=[blk, col, col, pl.BlockSpec((rb, rb), lambda i: (0, 0))],
        out_specs=[blk, blk],
        out_shape=[jax.ShapeDtypeStruct((E, T), jnp.int32)] * 2,
        scratch_shapes=[pltpu.VMEM((E, 1), F32), pltpu.VMEM((E, 1), F32)],
        compiler_params=_params(1),
    )(aff_t, thr, need, tri)
    slots = jnp.arange(cap, dtype=jnp.int32)
    group_end = csum[:, LANES - 1::LANES]
    grp = jnp.sum(group_end[:, None, :] <= slots[None, :, None], axis=-1, dtype=jnp.int32)
    rows = jnp.take_along_axis(csum.reshape(E, T // LANES, LANES), grp[:, :, None], axis=1)
    idx = grp * LANES + jnp.sum(rows <= slots[None, :, None], axis=-1, dtype=jnp.int32)
    gates = jnp.take_along_axis(aff_t, idx, axis=1)
    return idx, gates, pos, csum


def _moe(x, gate, h, logits, w_gate, w_up, w_down, layer, *, rows_per_seq):
    T, D = h.shape
    cap = EC_CAPACITY_FACTOR * T // N_EXPERTS
    idx, gates, pos, csum = _route(logits, cap)
    xe = h[idx]
    ye = expert_ffn(xe, w_gate, w_up, w_down, layer, gates[..., None], tm=2048, tf=512, td=256)
    n_tb = T // COMBINE_TOKENS
    ends = csum[:, COMBINE_TOKENS - 1::COMBINE_TOKENS].T
    starts = jnp.concatenate([jnp.zeros((1, N_EXPERTS), jnp.int32), ends[:-1]], axis=0)
    base = (starts // SLOT_BLOCK) * SLOT_BLOCK
    n_rounds = jnp.maximum(1, jnp.max(-(-(ends - base) // SLOT_WINDOW), axis=1)).astype(jnp.int32)
    pos_tm = jnp.pad(pos.T, ((0, 0), (0, LANES - N_EXPERTS)), constant_values=-1)
    return moe_combine(x, gate, ye, pos_tm, base.reshape(n_tb * N_EXPERTS), n_rounds,
                       rows_per_seq=rows_per_seq)


def _ada_modulation(c_all, w_ada, b_ada):
    n = c_all.shape[0]
    pad = (-n) % 16
    a = jnp.pad(jax.nn.silu(c_all), ((0, pad), (0, 0))).astype(BF16)
    return matmul_bias_layers(a, w_ada, b_ada[:, None, :], tn=1024)[:, :n]


def _to_classes(a, B, S, dilation):
    W = a.shape[1]
    if dilation == 1:
        return a.reshape(B, 1, S, W)
    return a.reshape(B, S // dilation, dilation, W).transpose(0, 2, 1, 3)


def _from_classes(a, B, S):
    _, dilation, L, W = a.shape
    if dilation == 1:
        return a.reshape(B * S, W)
    return a.transpose(0, 2, 1, 3).reshape(B * S, W)


def _run_trunk(x, mods, g_mix, g_ffn, w_fourier_out, w_qkv, w_attn_out, w_router_pad,
               w_gate, w_up, w_down, g_final, tables):
    B, S, D = x.shape
    T = B * S
    m_seq, m_ch, rope = tables
    x = x.reshape(T, D)
    depth = g_mix.shape[0]
    for i in range(depth):
        sh1, sc1, gt1, sh2, sc2, gt2 = [t[:, None, :] for t in jnp.split(mods[i], 6, axis=-1)]
        h = norm_modulate(x, g_mix[i][None, :], sh1, sc1, rows_per_seq=S, ts=512)
        if i % N_MIXERS == 0:
            z = dft_seq(m_seq, h.reshape(B, S, D), tm=1024)
            mixed = dft_channel(z.reshape(T, 2 * D), m_ch, tm=1024)
            x = matmul_residual(mixed, w_fourier_out, i // N_MIXERS, x, gt1, rows_per_seq=S, tm=2048, tn=512)
        else:
            outs, lses = [], []
            for g, (window, dilation) in enumerate(DILATION_GROUPS):
                qkv = matmul_qkv_rope(h, w_qkv, i // N_MIXERS, g, *rope, rows_per_seq=S, tm=2048, tn=512)
                o, lse = band_attention(_to_classes(qkv, B, S, dilation), dilation, window // (2 * dilation),
                                        B=B, S=S)
                outs.append(_from_classes(o, B, S))
                lses.append(_from_classes(lse, B, S))
            merged = attn_merge(outs, lses, tm=512)
            x = matmul_residual(merged, w_attn_out, i // N_MIXERS, x, gt1, rows_per_seq=S, tm=2048, tn=512)
        h, logits = norm_modulate(x, g_ffn[i][None, :], sh2, sc2, rows_per_seq=S, ts=512,
                                  w_router=w_router_pad[i])
        x = _moe(x, gt2, h, logits, w_gate, w_up, w_down, i, rows_per_seq=S)
    return final_norm(x, g_final[None, :], ts=512).reshape(B, S, D)


def kernel(x_prompt, x_sample, c_prompt, c_sample, g_mix, g_ffn, w_ada, b_ada, w_fourier_out, w_qkv,
           w_attn_out, w_router, w_gate, w_up, w_down, g_final):
    nb = x_prompt.shape[0]
    mods = _ada_modulation(jnp.concatenate([c_prompt, c_sample], axis=0), w_ada, b_ada)
    w_router_pad = jnp.pad(w_router, ((0, 0), (0, 0), (0, LANES - w_router.shape[2]))).astype(BF16)
    w_qkv = permute_qk_columns(w_qkv)
    w_fourier_out = w_fourier_out.astype(BF16)
    w_attn_out = w_attn_out.astype(BF16)
    outs = []
    table_cache = {}
    for x, lo, hi in ((x_prompt, 0, nb), (x_sample, nb, nb + x_sample.shape[0])):
        S, D = x.shape[1], x.shape[2]
        if S not in table_cache:
            table_cache[S] = dft_tables(S, D // FOURIER_GROUPS) + (rope_tables(S),)
        tables = table_cache[S]
        outs.append(_run_trunk(x, mods[:, lo:hi], g_mix, g_ffn, w_fourier_out, w_qkv,
                               w_attn_out, w_router_pad, w_gate, w_up, w_down, g_final, tables))
    return tuple(outs)
```

```python
import functools
import math

import jax
import jax.numpy as jnp
from jax import lax
from jax.experimental import pallas as pl
from jax.experimental.pallas import tpu as pltpu

BF16 = jnp.bfloat16
F32 = jnp.float32

N_MIXERS = 2
FOURIER_GROUPS = 4
HEAD_DIM = 128
HEADS_PER_GROUP = 8
DILATION_GROUPS = ((128, 1), (512, 4), (2048, 16))
N_ATTN_GROUPS = 3
ROT_DIM = HEAD_DIM // 4
ROPE_THETA = 500000.0
N_EXPERTS = 16
EC_CAPACITY_FACTOR = 2
NORM_EPS = 1e-6
NEG_INF = -1e30

V7X_VMEM_BYTES = 64 * 1024 * 1024
VMEM_LIMIT = V7X_VMEM_BYTES - 8 * 1024 * 1024
LANES = 128
LSE_REP = LANES // HEADS_PER_GROUP


def _params(n_axes):
    return pltpu.CompilerParams(dimension_semantics=("arbitrary",) * n_axes,
                                vmem_limit_bytes=VMEM_LIMIT)


def _tile(n, want):
    t = min(n, want)
    assert n % t == 0, (n, want)
    return t


def _mm_bias_kernel(a_ref, w_ref, b_ref, o_ref):
    acc = jnp.dot(a_ref[...], w_ref[0].astype(BF16), preferred_element_type=F32)
    o_ref[0] = (acc + b_ref[0]).astype(o_ref.dtype)


def _mm_residual_kernel(a_ref, w_ref, x_ref, g_ref, o_ref):
    acc = jnp.dot(a_ref[...], w_ref[0].astype(BF16), preferred_element_type=F32)
    o_ref[...] = x_ref[...] + g_ref[0] * acc


def matmul_bias_layers(a, w, b, *, tn):
    M, K = a.shape
    n_layers, _, N = w.shape
    tn = _tile(N, tn)
    return pl.pallas_call(
        _mm_bias_kernel,
        grid=(n_layers, N // tn),
        in_specs=[pl.BlockSpec((M, K), lambda l, j: (0, 0)),
                  pl.BlockSpec((1, K, tn), lambda l, j: (l, 0, j)),
                  pl.BlockSpec((1, 1, tn), lambda l, j: (l, 0, j))],
        out_specs=pl.BlockSpec((1, M, tn), lambda l, j: (l, 0, j)),
        out_shape=jax.ShapeDtypeStruct((n_layers, M, N), F32),
        compiler_params=_params(2),
    )(a, w, b)


def matmul_residual(a, w, layer, x, gate, *, rows_per_seq, tm, tn):
    T, K = a.shape
    N = w.shape[2]
    tm = _tile(rows_per_seq, tm)
    tn = _tile(N, tn)
    per_seq = rows_per_seq // tm
    return pl.pallas_call(
        _mm_residual_kernel,
        grid=(T // tm, N // tn),
        in_specs=[pl.BlockSpec((tm, K), lambda i, j: (i, 0)),
                  pl.BlockSpec((1, K, tn), lambda i, j: (layer, 0, j)),
                  pl.BlockSpec((tm, tn), lambda i, j: (i, j)),
                  pl.BlockSpec((1, 1, tn), lambda i, j: (i // per_seq, 0, j))],
        out_specs=pl.BlockSpec((tm, tn), lambda i, j: (i, j)),
        out_shape=jax.ShapeDtypeStruct((T, N), F32),
        compiler_params=_params(2),
    )(a, w, x, gate)


def permute_qk_columns(w_qkv):
    n_layers, D, N = w_qkv.shape
    w = w_qkv.reshape(n_layers, D, 3, N // (3 * HEAD_DIM), HEAD_DIM)
    half, mid = ROT_DIM // 2, HEAD_DIM // 2
    qk = w[:, :, :2]
    qk = jnp.concatenate([qk[..., :half], qk[..., ROT_DIM:ROT_DIM + mid - half],
                          qk[..., half:ROT_DIM], qk[..., ROT_DIM + mid - half:]], axis=-1)
    return jnp.concatenate([qk, w[:, :, 2:]], axis=2).reshape(n_layers, D, N).astype(BF16)


def rope_tables(S):
    half, mid = ROT_DIM // 2, HEAD_DIM // 2
    inv_freq = ROPE_THETA ** (-jnp.arange(half, dtype=F32) / half)
    ang = jnp.arange(S, dtype=F32)[:, None] * inv_freq[None, :]
    cos, sin = jnp.cos(ang), jnp.sin(ang)
    one = jnp.ones((S, mid - half), F32)
    zero = jnp.zeros((S, mid - half), F32)
    cos_t = jnp.concatenate([cos, one, cos, one], axis=1)
    sin_t = jnp.concatenate([-sin, zero, sin, zero], axis=1)
    cos_all = jnp.stack([cos_t, jnp.ones_like(cos_t)])
    sin_all = jnp.stack([sin_t, jnp.zeros_like(sin_t)])
    return cos_all, sin_all


def _qkv_rope_kernel(a_ref, w_ref, cos_ref, sin_ref, o_ref):
    acc = jnp.dot(a_ref[...], w_ref[0].astype(BF16), preferred_element_type=F32)
    cos, sin = cos_ref[0], sin_ref[0]
    for h in range(acc.shape[1] // HEAD_DIM):
        xh = acc[:, h * HEAD_DIM:(h + 1) * HEAD_DIM]
        rot = pltpu.roll(xh, HEAD_DIM // 2, 1)
        o_ref[0, :, h * HEAD_DIM:(h + 1) * HEAD_DIM] = (xh * cos + rot * sin).astype(o_ref.dtype)


def class_rope_tables(cos_all, sin_all, dilation, rows):
    S = cos_all.shape[1]
    L = S // dilation

    def view(t):
        t = t.reshape(2, L, dilation * HEAD_DIM)
        return jnp.tile(t, (1, rows // L, 1)) if rows > L else t

    return view(cos_all), view(sin_all)


def matmul_qkv_rope(hv, w_qkv, layer, group, dilation, cos_all, sin_all, *, rows_per_class, tm, tn):
    R, DD = hv.shape
    K = DD // dilation
    gw = HEADS_PER_GROUP * HEAD_DIM
    tm = _tile(R, tm)
    tn = _tile(gw, tn)
    n_per = gw // tn
    rows_tab = max(tm, rows_per_class)
    cos_d, sin_d = class_rope_tables(cos_all, sin_all, dilation, rows_tab)
    tab = pl.BlockSpec((1, tm, HEAD_DIM), lambda r, i, j: (j // (2 * n_per), i % (rows_tab // tm), r))
    return pl.pallas_call(
        _qkv_rope_kernel,
        grid=(dilation, R // tm, 3 * n_per),
        in_specs=[pl.BlockSpec((tm, K), lambda r, i, j: (i, r)),
                  pl.BlockSpec((1, K, tn), lambda r, i, j: (layer, 0, (j // n_per) * N_ATTN_GROUPS * n_per
                                                            + group * n_per + j % n_per)),
                  tab, tab],
        out_specs=pl.BlockSpec((1, tm, tn), lambda r, i, j: (r, i, j)),
        out_shape=jax.ShapeDtypeStruct((dilation, R, 3 * gw), BF16),
        compiler_params=_params(3),
    )(hv, w_qkv, cos_d, sin_d)


def _norm_mod_kernel(x_ref, g_ref, sh_ref, sc_ref, o_ref):
    x = x_ref[...]
    y = x * lax.rsqrt(jnp.mean(x * x, axis=-1, keepdims=True) + NORM_EPS) * g_ref[...]
    o_ref[...] = (y * (1.0 + sc_ref[0]) + sh_ref[0]).astype(o_ref.dtype)


def _norm_mod_router_kernel(x_ref, g_ref, sh_ref, sc_ref, wr_ref, o_ref, lg_ref):
    x = x_ref[...]
    y = x * lax.rsqrt(jnp.mean(x * x, axis=-1, keepdims=True) + NORM_EPS) * g_ref[...]
    h = (y * (1.0 + sc_ref[0]) + sh_ref[0]).astype(BF16)
    o_ref[...] = h
    lg_ref[...] = jnp.dot(h, wr_ref[...], preferred_element_type=F32)


def norm_modulate(x, g, shift, scale, *, rows_per_seq, ts, w_router=None):
    T, D = x.shape
    ts = _tile(rows_per_seq, ts)
    per_seq = rows_per_seq // ts
    row = pl.BlockSpec((ts, D), lambda i: (i, 0))
    vec = pl.BlockSpec((1, D), lambda i: (0, 0))
    seq = pl.BlockSpec((1, 1, D), lambda i: (i // per_seq, 0, 0))
    if w_router is None:
        return pl.pallas_call(
            _norm_mod_kernel, grid=(T // ts,),
            in_specs=[row, vec, seq, seq], out_specs=row,
            out_shape=jax.ShapeDtypeStruct((T, D), BF16),
            compiler_params=_params(1),
        )(x, g, shift, scale)
    return pl.pallas_call(
        _norm_mod_router_kernel, grid=(T // ts,),
        in_specs=[row, vec, seq, seq, pl.BlockSpec((D, LANES), lambda i: (0, 0))],
        out_specs=[row, pl.BlockSpec((ts, LANES), lambda i: (i, 0))],
        out_shape=[jax.ShapeDtypeStruct((T, D), BF16), jax.ShapeDtypeStruct((T, LANES), F32)],
        compiler_params=_params(1),
    )(x, g, shift, scale, w_router)


def _norm_mod_classes_kernel(x_ref, g_ref, sh_ref, sc_ref, *refs, dilations):
    out_refs, chunk_ref = refs[:-1], refs[-1]
    x = x_ref[...]
    y = x * lax.rsqrt(jnp.mean(x * x, axis=-1, keepdims=True) + NORM_EPS) * g_ref[...]
    y = y * (1.0 + sc_ref[0]) + sh_ref[0]
    ts, D = y.shape
    n_chunks = D // LANES
    for c in range(n_chunks):
        chunk_ref[c] = y[:, c * LANES:(c + 1) * LANES]
    for o_ref, d in zip(out_refs, dilations):
        if d == 1:
            o_ref[...] = y.astype(o_ref.dtype)
            continue
        for r in range(d):
            for c in range(n_chunks):
                o_ref[:, r * D + c * LANES:r * D + (c + 1) * LANES] = (
                    chunk_ref[c, pl.ds(r, ts // d, stride=d), :].astype(o_ref.dtype))


def norm_modulate_classes(x, g, shift, scale, dilations, *, rows_per_seq, ts):
    T, D = x.shape
    ts = _tile(rows_per_seq, ts)
    per_seq = rows_per_seq // ts
    row = pl.BlockSpec((ts, D), lambda i: (i, 0))
    vec = pl.BlockSpec((1, D), lambda i: (0, 0))
    seq = pl.BlockSpec((1, 1, D), lambda i: (i // per_seq, 0, 0))
    return pl.pallas_call(
        functools.partial(_norm_mod_classes_kernel, dilations=dilations), grid=(T // ts,),
        in_specs=[row, vec, seq, seq],
        out_specs=[pl.BlockSpec((ts // d, d * D), lambda i: (i, 0)) for d in dilations],
        out_shape=[jax.ShapeDtypeStruct((T // d, d * D), BF16) for d in dilations],
        scratch_shapes=[pltpu.VMEM((D // LANES, ts, LANES), F32)],
        compiler_params=_params(1),
    )(x, g, shift, scale)


def _final_norm_kernel(x_ref, g_ref, o_ref):
    x = x_ref[...]
    o_ref[...] = x * lax.rsqrt(jnp.mean(x * x, axis=-1, keepdims=True) + NORM_EPS) * g_ref[...]


def final_norm(x, g, *, ts):
    T, D = x.shape
    ts = _tile(T, ts)
    return pl.pallas_call(
        _final_norm_kernel, grid=(T // ts,),
        in_specs=[pl.BlockSpec((ts, D), lambda i: (i, 0)), pl.BlockSpec((1, D), lambda i: (0, 0))],
        out_specs=pl.BlockSpec((ts, D), lambda i: (i, 0)),
        out_shape=jax.ShapeDtypeStruct((T, D), F32),
        compiler_params=_params(1),
    )(x, g)


def dft_tables(S, C):
    r = int(round(math.sqrt(S)))
    assert r * r == S
    k = jnp.arange(S, dtype=jnp.int32)[:, None]
    n = jnp.arange(r, dtype=jnp.int32)[None, :]
    ang_hi = ((k * n * r) % S).astype(F32) * (2.0 * math.pi / S)
    ang_lo = ((k * n) % S).astype(F32) * (2.0 * math.pi / S)
    ch, sh_ = jnp.cos(ang_hi)[:, :, None], jnp.sin(ang_hi)[:, :, None]
    cl, sl = jnp.cos(ang_lo)[:, None, :], jnp.sin(ang_lo)[:, None, :]
    scale = float(S) ** -0.5
    cos_s = ((ch * cl - sh_ * sl) * scale).reshape(S, S)
    sin_s = ((sh_ * cl + ch * sl) * scale).reshape(S, S)
    m_seq = jnp.concatenate([cos_s, sin_s], axis=0).astype(BF16)
    kc = jnp.arange(C, dtype=jnp.int32)
    ang_c = ((kc[:, None] * kc[None, :]) % C).astype(F32) * (2.0 * math.pi / C)
    cscale = float(C) ** -0.5
    m_ch = jnp.concatenate([jnp.cos(ang_c) * cscale, -jnp.sin(ang_c) * cscale], axis=0).astype(BF16)
    return m_seq, m_ch


def _dft_seq_kernel(m_ref, h_ref, o_ref):
    o_ref[0] = jnp.dot(m_ref[...], h_ref[0], preferred_element_type=F32).astype(o_ref.dtype)


def dft_seq(m_seq, h3, *, tm):
    B, S, D = h3.shape
    C = D // FOURIER_GROUPS
    tm = _tile(S, tm)
    n_i = S // tm
    return pl.pallas_call(
        _dft_seq_kernel,
        grid=(2 * n_i, B, FOURIER_GROUPS),
        in_specs=[pl.BlockSpec((tm, S), lambda i, b, g: (i, 0)),
                  pl.BlockSpec((1, S, C), lambda i, b, g: (b, 0, g))],
        out_specs=pl.BlockSpec((1, tm, C), lambda i, b, g: (b, i % n_i, 2 * g + i // n_i)),
        out_shape=jax.ShapeDtypeStruct((B, S, 2 * D), BF16),
        compiler_params=_params(3),
    )(m_seq, h3)


def _mm_const_kernel(a_ref, w_ref, o_ref):
    o_ref[...] = jnp.dot(a_ref[...], w_ref[...], preferred_element_type=F32).astype(o_ref.dtype)


def dft_channel(z, m_ch, *, tm):
    T = z.shape[0]
    C2, C = m_ch.shape
    tm = _tile(T, tm)
    return pl.pallas_call(
        _mm_const_kernel,
        grid=(T // tm, FOURIER_GROUPS),
        in_specs=[pl.BlockSpec((tm, C2), lambda i, g: (i, g)),
                  pl.BlockSpec((C2, C), lambda i, g: (0, 0))],
        out_specs=pl.BlockSpec((tm, C), lambda i, g: (i, g)),
        out_shape=jax.ShapeDtypeStruct((T, FOURIER_GROUPS * C), BF16),
        compiler_params=_params(2),
    )(z, m_ch)


def _band_attn_kernel(q_ref, k_ref, v_ref, o_ref, lse_ref, *, L, tq, win, half, heads):
    hb = pl.program_id(2)
    scale = 1.0 / math.sqrt(HEAD_DIM)
    lane_head = lax.broadcasted_iota(jnp.int32, (tq, LANES), 1) // LSE_REP

    @pl.when(hb == 0)
    def _():
        lse_ref[...] = jnp.zeros(lse_ref.shape, lse_ref.dtype)

    def body(qi, carry):
        q0 = pl.multiple_of(qi * tq, tq)
        ks = pl.multiple_of(jnp.clip(q0 - half, 0, L - win), half)
        qpos = q0 + lax.broadcasted_iota(jnp.int32, (tq, win), 0)
        kpos = ks + lax.broadcasted_iota(jnp.int32, (tq, win), 1)
        band = jnp.abs(qpos - kpos) <= half
        lse_tile = lse_ref[0, 0, pl.ds(q0, tq), :]
        for h in range(heads):
            cs = slice(h * HEAD_DIM, (h + 1) * HEAD_DIM)
            q = q_ref[0, 0, pl.ds(q0, tq), cs]
            k = k_ref[0, 0, pl.ds(ks, win), cs]
            v = v_ref[0, 0, pl.ds(ks, win), cs]
            s = lax.dot_general(q, k, (((1,), (1,)), ((), ())), preferred_element_type=F32) * scale
            s = jnp.where(band, s, NEG_INF)
            m = jnp.max(s, axis=-1, keepdims=True)
            p = jnp.exp(s - m)
            denom = jnp.sum(p, axis=-1, keepdims=True)
            o = jnp.dot(p.astype(BF16), v, preferred_element_type=F32) / denom
            o_ref[0, 0, pl.ds(q0, tq), cs] = o.astype(o_ref.dtype)
            lse_tile = jnp.where(lane_head == hb * heads + h, m + jnp.log(denom), lse_tile)
        lse_ref[0, 0, pl.ds(q0, tq), :] = lse_tile
        return carry

    lax.fori_loop(0, L // tq, body, 0)


ATTN_BLOCK_BYTES = 4 * 1024 * 1024


def band_attention(qkv, dilation, half, *, B, S):
    L = S // dilation
    H = HEADS_PER_GROUP
    tq = min(2 * half, L)
    win = min(tq + 2 * half, L)
    heads_per_step = max(n for n in (1, 2, 4, 8) if H % n == 0 and (n == 1 or L * n * HEAD_DIM * 2 <= ATTN_BLOCK_BYTES))
    hw = heads_per_step * HEAD_DIM
    n_hb = H // heads_per_step

    def spec(which):
        return pl.BlockSpec((1, 1, L, hw), lambda b, r, h: (r, b, 0, which * n_hb + h))

    return pl.pallas_call(
        functools.partial(_band_attn_kernel, L=L, tq=tq, win=win, half=half, heads=heads_per_step),
        grid=(B, dilation, n_hb),
        in_specs=[spec(0), spec(1), spec(2)],
        out_specs=[pl.BlockSpec((1, 1, L, hw), lambda b, r, h: (r, b, 0, h)),
                   pl.BlockSpec((1, 1, L, LANES), lambda b, r, h: (r, b, 0, 0))],
        out_shape=[jax.ShapeDtypeStruct((dilation, B, L, H * HEAD_DIM), BF16),
                   jax.ShapeDtypeStruct((dilation, B, L, LANES), F32)],
        compiler_params=_params(3),
    )(qkv, qkv, qkv)


def _attn_merge_kernel(*refs, dilations):
    n = len(dilations)
    o_refs, l_refs, out_ref = refs[:n], refs[n:2 * n], refs[2 * n]
    scratch = refs[2 * n + 1:]
    tm = out_ref.shape[0]
    outs, lses, k = [], [], 0
    for o_ref, l_ref, d in zip(o_refs, l_refs, dilations):
        if d == 1:
            outs.append(lambda h, o_ref=o_ref: o_ref[0, 0, :, h * HEAD_DIM:(h + 1) * HEAD_DIM].astype(F32))
            lses.append(l_ref[0, 0])
            continue
        so_ref, sl_ref = scratch[k], scratch[k + 1]
        k += 2
        for r in range(d):
            sl_ref[0, pl.ds(r, tm // d, stride=d), :] = l_ref[r, 0]
            for h in range(HEADS_PER_GROUP):
                so_ref[h, pl.ds(r, tm // d, stride=d), :] = (
                    o_ref[r, 0, :, h * HEAD_DIM:(h + 1) * HEAD_DIM].astype(F32))
        outs.append(lambda h, so_ref=so_ref: so_ref[h])
        lses.append(sl_ref[0])
    m = functools.reduce(jnp.maximum, lses)
    es = [jnp.exp(l - m) for l in lses]
    inv = 1.0 / functools.reduce(lambda a, b: a + b, es)
    ws = [e * inv for e in es]
    for h in range(HEADS_PER_GROUP):
        c = h * LSE_REP
        acc = functools.reduce(lambda a, b: a + b, [w[:, c:c + 1] * o(h) for w, o in zip(ws, outs)])
        out_ref[:, h * HEAD_DIM:(h + 1) * HEAD_DIM] = acc.astype(out_ref.dtype)


def attn_merge(outs, lses, dilations, *, B, S, tm):
    W = outs[0].shape[3]
    tm = _tile(S, tm)
    per_seq = S // tm
    o_specs = [pl.BlockSpec((d, 1, tm // d, W), lambda b, i: (0, b, i, 0)) for d in dilations]
    l_specs = [pl.BlockSpec((d, 1, tm // d, LANES), lambda b, i: (0, b, i, 0)) for d in dilations]
    scratch = []
    for d in dilations:
        if d > 1:
            scratch += [pltpu.VMEM((W // HEAD_DIM, tm, HEAD_DIM), F32), pltpu.VMEM((1, tm, LANES), F32)]
    return pl.pallas_call(
        functools.partial(_attn_merge_kernel, dilations=dilations), grid=(B, per_seq),
        in_specs=o_specs + l_specs,
        out_specs=pl.BlockSpec((tm, W), lambda b, i: (b * per_seq + i, 0)),
        out_shape=jax.ShapeDtypeStruct((B * S, W), BF16),
        scratch_shapes=scratch,
        compiler_params=_params(2),
    )(*outs, *lses)


def _ffn_up_kernel(x_ref, wg_ref, wu_ref, o_ref):
    x = x_ref[0]
    g = jnp.dot(x, wg_ref[0, 0].astype(BF16), preferred_element_type=F32)
    u = jnp.dot(x, wu_ref[0, 0].astype(BF16), preferred_element_type=F32)
    o_ref[0] = (g * jax.nn.sigmoid(g) * u).astype(o_ref.dtype)


def _ffn_down_kernel(h_ref, wd_ref, gate_ref, o_ref):
    acc = jnp.dot(h_ref[0], wd_ref[0, 0].astype(BF16), preferred_element_type=F32)
    o_ref[0] = (acc * gate_ref[0]).astype(o_ref.dtype)


def expert_ffn(xe, w_gate, w_up, w_down, layer, gates, *, tm, tf, td):
    E, C, D = xe.shape
    F = w_gate.shape[3]
    tm = _tile(C, tm)
    tf = _tile(F, tf)
    td = _tile(D, td)
    hid = pl.pallas_call(
        _ffn_up_kernel,
        grid=(E, C // tm, F // tf),
        in_specs=[pl.BlockSpec((1, tm, D), lambda e, i, f: (e, i, 0)),
                  pl.BlockSpec((1, 1, D, tf), lambda e, i, f: (layer, e, 0, f)),
                  pl.BlockSpec((1, 1, D, tf), lambda e, i, f: (layer, e, 0, f))],
        out_specs=pl.BlockSpec((1, tm, tf), lambda e, i, f: (e, i, f)),
        out_shape=jax.ShapeDtypeStruct((E, C, F), BF16),
        compiler_params=_params(3),
    )(xe, w_gate, w_up)
    return pl.pallas_call(
        _ffn_down_kernel,
        grid=(E, C // tm, D // td),
        in_specs=[pl.BlockSpec((1, tm, F), lambda e, i, j: (e, i, 0)),
                  pl.BlockSpec((1, 1, F, td), lambda e, i, j: (layer, e, 0, j)),
                  pl.BlockSpec((1, tm, 1), lambda e, i, j: (e, i, 0))],
        out_specs=pl.BlockSpec((1, tm, td), lambda e, i, j: (e, i, j)),
        out_shape=jax.ShapeDtypeStruct((E, C, D), BF16),
        compiler_params=_params(3),
    )(hid, w_down, gates)


COMBINE_TOKENS = 256
SLOT_BLOCK = 64
SLOT_WINDOW = 2 * SLOT_BLOCK


def _combine_kernel(base_ref, nr_ref, x_ref, gt_ref, pos_ref, *refs, n_exp):
    y_refs, o_ref, acc_ref = refs[:2 * n_exp], refs[2 * n_exp], refs[2 * n_exp + 1]
    tb, r = pl.program_id(0), pl.program_id(1)
    n_tok = x_ref.shape[0]

    @pl.when(r == 0)
    def _():
        acc_ref[...] = jnp.zeros(acc_ref.shape, acc_ref.dtype)

    @pl.when(r < nr_ref[tb])
    def _():
        lane = lax.broadcasted_iota(jnp.int32, (n_tok, SLOT_WINDOW), 1)
        pos = pos_ref[...]
        onehots = []
        for e in range(n_exp):
            start = base_ref[tb * n_exp + e] + r * SLOT_WINDOW
            onehots.append(jnp.where(pos[:, e:e + 1] - start == lane, 1.0, 0.0).astype(BF16))
        onehot = jnp.concatenate(onehots, axis=1)
        rows = jnp.concatenate([y[0] for y in y_refs], axis=0)
        acc_ref[...] += jnp.dot(onehot, rows, preferred_element_type=F32)

    @pl.when(r == pl.num_programs(1) - 1)
    def _():
        o_ref[...] = x_ref[...] + gt_ref[0] * acc_ref[...]


def moe_combine(x, gate, ye, pos_tm, base, n_rounds, *, rows_per_seq):
    T, D = x.shape
    E, C, _ = ye.shape
    tb = COMBINE_TOKENS
    per_seq = rows_per_seq // tb
    max_rounds = -(-(SLOT_BLOCK - 1 + tb) // SLOT_WINDOW)
    last_blk = C // SLOT_BLOCK - 1

    def y_spec(e, k):
        def imap(t, r, base_ref, nr_ref):
            rr = jnp.minimum(r, nr_ref[t] - 1)
            blk = base_ref[t * E + e] // SLOT_BLOCK + 2 * rr + k
            return (e, jnp.minimum(blk, last_blk), 0)
        return pl.BlockSpec((1, SLOT_BLOCK, D), imap)

    grid_spec = pltpu.PrefetchScalarGridSpec(
        num_scalar_prefetch=2,
        grid=(T // tb, max_rounds),
        in_specs=[pl.BlockSpec((tb, D), lambda t, r, b, n: (t, 0)),
                  pl.BlockSpec((1, 1, D), lambda t, r, b, n: (t // per_seq, 0, 0)),
                  pl.BlockSpec((tb, LANES), lambda t, r, b, n: (t, 0))]
                 + [y_spec(e, k) for e in range(E) for k in range(2)],
        out_specs=pl.BlockSpec((tb, D), lambda t, r, b, n: (t, 0)),
        scratch_shapes=[pltpu.VMEM((tb, D), F32)],
    )
    return pl.pallas_call(
        functools.partial(_combine_kernel, n_exp=E),
        grid_spec=grid_spec,
        out_shape=jax.ShapeDtypeStruct((T, D), F32),
        compiler_params=_params(2),
    )(base, n_rounds, x, gate, pos_tm, *([ye] * (2 * E)))


ROUTE_BLOCK = 512


def _route_threshold_kernel(aff_ref, thr_ref, need_ref, *, cap):
    bits = pltpu.bitcast(aff_ref[...], jnp.int32)

    def step(i, v):
        cand = v | (1 << (30 - i))
        cnt = jnp.sum(jnp.where(bits >= cand, 1, 0), axis=1, keepdims=True)
        return jnp.where(cnt >= cap, cand, v)

    v = lax.fori_loop(0, 31, step, jnp.zeros((bits.shape[0], 1), jnp.int32))
    thr_ref[...] = v
    need_ref[...] = cap - jnp.sum(jnp.where(bits > v, 1, 0), axis=1, keepdims=True)


def _route_scan_kernel(aff_ref, thr_ref, need_ref, tri_ref, pos_ref, csum_ref, tied_acc, sel_acc):
    @pl.when(pl.program_id(0) == 0)
    def _():
        tied_acc[...] = jnp.zeros(tied_acc.shape, tied_acc.dtype)
        sel_acc[...] = jnp.zeros(sel_acc.shape, sel_acc.dtype)

    bits = pltpu.bitcast(aff_ref[...], jnp.int32)
    thr = thr_ref[...]
    tri = tri_ref[...]
    tied = bits == thr
    tied_count = tied_acc[...] + jnp.dot(jnp.where(tied, 1.0, 0.0).astype(BF16), tri, preferred_element_type=F32)
    sel = (bits > thr) | (tied & (tied_count.astype(jnp.int32) - 1 < need_ref[...]))
    sel_count = sel_acc[...] + jnp.dot(jnp.where(sel, 1.0, 0.0).astype(BF16), tri, preferred_element_type=F32)
    csum = sel_count.astype(jnp.int32)
    pos_ref[...] = jnp.where(sel, csum - 1, -1)
    csum_ref[...] = csum
    tied_acc[...] = tied_count[:, -1:]
    sel_acc[...] = sel_count[:, -1:]


def _route(logits, cap):
    aff_t = jax.nn.softmax(logits[:, :N_EXPERTS], axis=-1).T
    E, T = aff_t.shape
    thr, need = pl.pallas_call(
        functools.partial(_route_threshold_kernel, cap=cap),
        out_shape=[jax.ShapeDtypeStruct((E, 1), jnp.int32)] * 2,
        compiler_params=pltpu.CompilerParams(vmem_limit_bytes=VMEM_LIMIT),
    )(aff_t)
    rb = _tile(T, ROUTE_BLOCK)
    tri = (jnp.arange(rb)[:, None] <= jnp.arange(rb)[None, :]).astype(BF16)
    col = pl.BlockSpec((E, 1), lambda i: (0, 0))
    blk = pl.BlockSpec((E, rb), lambda i: (0, i))
    pos, csum = pl.pallas_call(
        _route_scan_kernel,
        grid=(T // rb,),
        in_specs=[blk, col, col, pl.BlockSpec((rb, rb), lambda i: (0, 0))],
        out_specs=[blk, blk],
        out_shape=[jax.ShapeDtypeStruct((E, T), jnp.int32)] * 2,
        scratch_shapes=[pltpu.VMEM((E, 1), F32), pltpu.VMEM((E, 1), F32)],
        compiler_params=_params(1),
    )(aff_t, thr, need, tri)
    slots = jnp.arange(cap, dtype=jnp.int32)
    group_end = csum[:, LANES - 1::LANES]
    grp = jnp.sum(group_end[:, None, :] <= slots[None, :, None], axis=-1, dtype=jnp.int32)
    pick = (grp[:, :, None] == jnp.arange(T // LANES, dtype=jnp.int32)[None, None, :]).astype(F32)
    rows = jnp.einsum("esg,egl->esl", pick, csum.reshape(E, T // LANES, LANES).astype(F32),
                      precision=lax.Precision.HIGHEST)
    idx = grp * LANES + jnp.sum(rows <= slots[None, :, None].astype(F32), axis=-1, dtype=jnp.int32)
    gates = jnp.take_along_axis(aff_t, idx, axis=1)
    return idx, gates, pos, csum


def _moe(x, gate, h, logits, w_gate, w_up, w_down, layer, *, rows_per_seq):
    T, D = h.shape
    cap = EC_CAPACITY_FACTOR * T // N_EXPERTS
    idx, gates, pos, csum = _route(logits, cap)
    xe = h[idx]
    ye = expert_ffn(xe, w_gate, w_up, w_down, layer, gates[..., None], tm=2048, tf=512, td=256)
    n_tb = T // COMBINE_TOKENS
    ends = csum[:, COMBINE_TOKENS - 1::COMBINE_TOKENS].T
    starts = jnp.concatenate([jnp.zeros((1, N_EXPERTS), jnp.int32), ends[:-1]], axis=0)
    base = (starts // SLOT_BLOCK) * SLOT_BLOCK
    n_rounds = jnp.maximum(1, jnp.max(-(-(ends - base) // SLOT_WINDOW), axis=1)).astype(jnp.int32)
    pos_tm = jnp.pad(pos.T, ((0, 0), (0, LANES - N_EXPERTS)), constant_values=-1)
    return moe_combine(x, gate, ye, pos_tm, base.reshape(n_tb * N_EXPERTS), n_rounds,
                       rows_per_seq=rows_per_seq)


def _ada_modulation(c_all, w_ada, b_ada):
    n = c_all.shape[0]
    pad = (-n) % 16
    a = jnp.pad(jax.nn.silu(c_all), ((0, pad), (0, 0))).astype(BF16)
    return matmul_bias_layers(a, w_ada, b_ada[:, None, :], tn=1024)[:, :n]


def _run_trunk(x, mods, g_mix, g_ffn, w_fourier_out, w_qkv, w_attn_out, w_router_pad,
               w_gate, w_up, w_down, g_final, tables):
    B, S, D = x.shape
    T = B * S
    m_seq, m_ch, rope = tables
    x = x.reshape(T, D)
    depth = g_mix.shape[0]
    for i in range(depth):
        sh1, sc1, gt1, sh2, sc2, gt2 = [t[:, None, :] for t in jnp.split(mods[i], 6, axis=-1)]
        if i % N_MIXERS == 0:
            h = norm_modulate(x, g_mix[i][None, :], sh1, sc1, rows_per_seq=S, ts=512)
            z = dft_seq(m_seq, h.reshape(B, S, D), tm=1024)
            mixed = dft_channel(z.reshape(T, 2 * D), m_ch, tm=1024)
            x = matmul_residual(mixed, w_fourier_out, i // N_MIXERS, x, gt1, rows_per_seq=S, tm=2048, tn=512)
        else:
            dilations = tuple(d for _, d in DILATION_GROUPS)
            hvs = norm_modulate_classes(x, g_mix[i][None, :], sh1, sc1, dilations, rows_per_seq=S, ts=512)
            outs, lses = [], []
            for g, (window, dilation) in enumerate(DILATION_GROUPS):
                L = S // dilation
                qkv = matmul_qkv_rope(hvs[g], w_qkv, i // N_MIXERS, g, dilation, *rope,
                                      rows_per_class=L, tm=2048, tn=512)
                o, lse = band_attention(qkv.reshape(dilation, B, L, qkv.shape[2]), dilation,
                                        window // (2 * dilation), B=B, S=S)
                outs.append(o)
                lses.append(lse)
            merged = attn_merge(outs, lses, dilations, B=B, S=S, tm=512)
            x = matmul_residual(merged, w_attn_out, i // N_MIXERS, x, gt1, rows_per_seq=S, tm=2048, tn=512)
        h, logits = norm_modulate(x, g_ffn[i][None, :], sh2, sc2, rows_per_seq=S, ts=512,
                                  w_router=w_router_pad[i])
        x = _moe(x, gt2, h, logits, w_gate, w_up, w_down, i, rows_per_seq=S)
    return final_norm(x, g_final[None, :], ts=512).reshape(B, S, D)


def kernel(x_prompt, x_sample, c_prompt, c_sample, g_mix, g_ffn, w_ada, b_ada, w_fourier_out, w_qkv,
           w_attn_out, w_router, w_gate, w_up, w_down, g_final):
    nb = x_prompt.shape[0]
    mods = _ada_modulation(jnp.concatenate([c_prompt, c_sample], axis=0), w_ada, b_ada)
    w_router_pad = jnp.pad(w_router, ((0, 0), (0, 0), (0, LANES - w_router.shape[2]))).astype(BF16)
    w_qkv = permute_qk_columns(w_qkv)
    w_fourier_out = w_fourier_out.astype(BF16)
    w_attn_out = w_attn_out.astype(BF16)
    outs = []
    table_cache = {}
    for x, lo, hi in ((x_prompt, 0, nb), (x_sample, nb, nb + x_sample.shape[0])):
        S, D = x.shape[1], x.shape[2]
        if S not in table_cache:
            table_cache[S] = dft_tables(S, D // FOURIER_GROUPS) + (rope_tables(S),)
        tables = table_cache[S]
        outs.append(_run_trunk(x, mods[:, lo:hi], g_mix, g_ffn, w_fourier_out, w_qkv,
                               w_attn_out, w_router_pad, w_gate, w_up, w_down, g_final, tables))
    return tuple(outs)
```

```python
import functools
import math

import jax
import jax.numpy as jnp
from jax import lax
from jax.experimental import pallas as pl
from jax.experimental.pallas import tpu as pltpu

BF16 = jnp.bfloat16
F32 = jnp.float32

N_MIXERS = 2
FOURIER_GROUPS = 4
HEAD_DIM = 128
HEADS_PER_GROUP = 8
DILATION_GROUPS = ((128, 1), (512, 4), (2048, 16))
N_ATTN_GROUPS = 3
ROT_DIM = HEAD_DIM // 4
ROPE_THETA = 500000.0
N_EXPERTS = 16
EC_CAPACITY_FACTOR = 2
NORM_EPS = 1e-6
NEG_INF = -1e30

V7X_VMEM_BYTES = 64 * 1024 * 1024
VMEM_LIMIT = V7X_VMEM_BYTES - 8 * 1024 * 1024
LANES = 128
LSE_REP = LANES // HEADS_PER_GROUP


def _params(n_axes):
    return pltpu.CompilerParams(dimension_semantics=("arbitrary",) * n_axes,
                                vmem_limit_bytes=VMEM_LIMIT)


def _tile(n, want):
    t = min(n, want)
    assert n % t == 0, (n, want)
    return t


def _mm_bias_kernel(a_ref, w_ref, b_ref, o_ref):
    acc = jnp.dot(a_ref[...], w_ref[0].astype(BF16), preferred_element_type=F32)
    o_ref[0] = (acc + b_ref[0]).astype(o_ref.dtype)


def _mm_residual_kernel(a_ref, w_ref, x_ref, g_ref, o_ref):
    acc = jnp.dot(a_ref[...], w_ref[0].astype(BF16), preferred_element_type=F32)
    o_ref[...] = x_ref[...] + g_ref[0] * acc


def matmul_bias_layers(a, w, b, *, tn):
    M, K = a.shape
    n_layers, _, N = w.shape
    tn = _tile(N, tn)
    return pl.pallas_call(
        _mm_bias_kernel,
        grid=(n_layers, N // tn),
        in_specs=[pl.BlockSpec((M, K), lambda l, j: (0, 0)),
                  pl.BlockSpec((1, K, tn), lambda l, j: (l, 0, j)),
                  pl.BlockSpec((1, 1, tn), lambda l, j: (l, 0, j))],
        out_specs=pl.BlockSpec((1, M, tn), lambda l, j: (l, 0, j)),
        out_shape=jax.ShapeDtypeStruct((n_layers, M, N), F32),
        compiler_params=_params(2),
    )(a, w, b)


def matmul_residual(a, w, layer, x, gate, *, rows_per_seq, tm, tn):
    T, K = a.shape
    N = w.shape[2]
    tm = _tile(rows_per_seq, tm)
    tn = _tile(N, tn)
    per_seq = rows_per_seq // tm
    return pl.pallas_call(
        _mm_residual_kernel,
        grid=(T // tm, N // tn),
        in_specs=[pl.BlockSpec((tm, K), lambda i, j: (i, 0)),
                  pl.BlockSpec((1, K, tn), lambda i, j: (layer, 0, j)),
                  pl.BlockSpec((tm, tn), lambda i, j: (i, j)),
                  pl.BlockSpec((1, 1, tn), lambda i, j: (i // per_seq, 0, j))],
        out_specs=pl.BlockSpec((tm, tn), lambda i, j: (i, j)),
        out_shape=jax.ShapeDtypeStruct((T, N), F32),
        compiler_params=_params(2),
    )(a, w, x, gate)


def permute_qk_columns(w_qkv):
    n_layers, D, N = w_qkv.shape
    w = w_qkv.reshape(n_layers, D, 3, N // (3 * HEAD_DIM), HEAD_DIM)
    half, mid = ROT_DIM // 2, HEAD_DIM // 2
    qk = w[:, :, :2]
    qk = jnp.concatenate([qk[..., :half], qk[..., ROT_DIM:ROT_DIM + mid - half],
                          qk[..., half:ROT_DIM], qk[..., ROT_DIM + mid - half:]], axis=-1)
    return jnp.concatenate([qk, w[:, :, 2:]], axis=2).reshape(n_layers, D, N).astype(BF16)


def rope_tables(S):
    half, mid = ROT_DIM // 2, HEAD_DIM // 2
    inv_freq = ROPE_THETA ** (-jnp.arange(half, dtype=F32) / half)
    ang = jnp.arange(S, dtype=F32)[:, None] * inv_freq[None, :]
    cos, sin = jnp.cos(ang), jnp.sin(ang)
    one = jnp.ones((S, mid - half), F32)
    zero = jnp.zeros((S, mid - half), F32)
    cos_t = jnp.concatenate([cos, one, cos, one], axis=1)
    sin_t = jnp.concatenate([-sin, zero, sin, zero], axis=1)
    cos_all = jnp.stack([cos_t, jnp.ones_like(cos_t)])
    sin_all = jnp.stack([sin_t, jnp.zeros_like(sin_t)])
    return cos_all, sin_all


def _qkv_rope_kernel(a_ref, w_ref, cos_ref, sin_ref, o_ref):
    acc = jnp.dot(a_ref[...], w_ref[0].astype(BF16), preferred_element_type=F32)
    cos, sin = cos_ref[0], sin_ref[0]
    for h in range(acc.shape[1] // HEAD_DIM):
        xh = acc[:, h * HEAD_DIM:(h + 1) * HEAD_DIM]
        rot = pltpu.roll(xh, HEAD_DIM // 2, 1)
        o_ref[0, :, h * HEAD_DIM:(h + 1) * HEAD_DIM] = (xh * cos + rot * sin).astype(o_ref.dtype)


def class_rope_tables(cos_all, sin_all, dilation, rows):
    S = cos_all.shape[1]
    L = S // dilation

    def view(t):
        t = t.reshape(2, L, dilation * HEAD_DIM)
        return jnp.tile(t, (1, rows // L, 1)) if rows > L else t

    return view(cos_all), view(sin_all)


def matmul_qkv_rope(hv, w_qkv, layer, group, dilation, cos_all, sin_all, *, rows_per_class, tm, tn):
    R, DD = hv.shape
    K = DD // dilation
    gw = HEADS_PER_GROUP * HEAD_DIM
    tm = _tile(R, tm)
    tn = _tile(gw, tn)
    n_per = gw // tn
    rows_tab = max(tm, rows_per_class)
    cos_d, sin_d = class_rope_tables(cos_all, sin_all, dilation, rows_tab)
    tab = pl.BlockSpec((1, tm, HEAD_DIM), lambda r, i, j: (j // (2 * n_per), i % (rows_tab // tm), r))
    return pl.pallas_call(
        _qkv_rope_kernel,
        grid=(dilation, R // tm, 3 * n_per),
        in_specs=[pl.BlockSpec((tm, K), lambda r, i, j: (i, r)),
                  pl.BlockSpec((1, K, tn), lambda r, i, j: (layer, 0, (j // n_per) * N_ATTN_GROUPS * n_per
                                                            + group * n_per + j % n_per)),
                  tab, tab],
        out_specs=pl.BlockSpec((1, tm, tn), lambda r, i, j: (r, i, j)),
        out_shape=jax.ShapeDtypeStruct((dilation, R, 3 * gw), BF16),
        compiler_params=_params(3),
    )(hv, w_qkv, cos_d, sin_d)


def _norm_mod_kernel(x_ref, g_ref, sh_ref, sc_ref, o_ref):
    x = x_ref[...]
    y = x * lax.rsqrt(jnp.mean(x * x, axis=-1, keepdims=True) + NORM_EPS) * g_ref[...]
    o_ref[...] = (y * (1.0 + sc_ref[0]) + sh_ref[0]).astype(o_ref.dtype)


def _norm_mod_router_kernel(x_ref, g_ref, sh_ref, sc_ref, wr_ref, o_ref, lg_ref):
    x = x_ref[...]
    y = x * lax.rsqrt(jnp.mean(x * x, axis=-1, keepdims=True) + NORM_EPS) * g_ref[...]
    h = (y * (1.0 + sc_ref[0]) + sh_ref[0]).astype(BF16)
    o_ref[...] = h
    lg_ref[...] = jnp.dot(h, wr_ref[...], preferred_element_type=F32)


def norm_modulate(x, g, shift, scale, *, rows_per_seq, ts, w_router=None):
    T, D = x.shape
    ts = _tile(rows_per_seq, ts)
    per_seq = rows_per_seq // ts
    row = pl.BlockSpec((ts, D), lambda i: (i, 0))
    vec = pl.BlockSpec((1, D), lambda i: (0, 0))
    seq = pl.BlockSpec((1, 1, D), lambda i: (i // per_seq, 0, 0))
    if w_router is None:
        return pl.pallas_call(
            _norm_mod_kernel, grid=(T // ts,),
            in_specs=[row, vec, seq, seq], out_specs=row,
            out_shape=jax.ShapeDtypeStruct((T, D), BF16),
            compiler_params=_params(1),
        )(x, g, shift, scale)
    return pl.pallas_call(
        _norm_mod_router_kernel, grid=(T // ts,),
        in_specs=[row, vec, seq, seq, pl.BlockSpec((D, LANES), lambda i: (0, 0))],
        out_specs=[row, pl.BlockSpec((ts, LANES), lambda i: (i, 0))],
        out_shape=[jax.ShapeDtypeStruct((T, D), BF16), jax.ShapeDtypeStruct((T, LANES), F32)],
        compiler_params=_params(1),
    )(x, g, shift, scale, w_router)


def _norm_mod_classes_kernel(x_ref, g_ref, sh_ref, sc_ref, *refs, dilations):
    out_refs, chunk_ref = refs[:-1], refs[-1]
    x = x_ref[...]
    y = x * lax.rsqrt(jnp.mean(x * x, axis=-1, keepdims=True) + NORM_EPS) * g_ref[...]
    y = y * (1.0 + sc_ref[0]) + sh_ref[0]
    ts, D = y.shape
    n_chunks = D // LANES
    for c in range(n_chunks):
        chunk_ref[c] = y[:, c * LANES:(c + 1) * LANES]
    for o_ref, d in zip(out_refs, dilations):
        if d == 1:
            o_ref[...] = y.astype(o_ref.dtype)
            continue
        for r in range(d):
            for c in range(n_chunks):
                o_ref[:, r * D + c * LANES:r * D + (c + 1) * LANES] = (
                    chunk_ref[c, pl.ds(r, ts // d, stride=d), :].astype(o_ref.dtype))


def norm_modulate_classes(x, g, shift, scale, dilations, *, rows_per_seq, ts):
    T, D = x.shape
    ts = _tile(rows_per_seq, ts)
    per_seq = rows_per_seq // ts
    row = pl.BlockSpec((ts, D), lambda i: (i, 0))
    vec = pl.BlockSpec((1, D), lambda i: (0, 0))
    seq = pl.BlockSpec((1, 1, D), lambda i: (i // per_seq, 0, 0))
    return pl.pallas_call(
        functools.partial(_norm_mod_classes_kernel, dilations=dilations), grid=(T // ts,),
        in_specs=[row, vec, seq, seq],
        out_specs=[pl.BlockSpec((ts // d, d * D), lambda i: (i, 0)) for d in dilations],
        out_shape=[jax.ShapeDtypeStruct((T // d, d * D), BF16) for d in dilations],
        scratch_shapes=[pltpu.VMEM((D // LANES, ts, LANES), F32)],
        compiler_params=_params(1),
    )(x, g, shift, scale)


def _final_norm_kernel(x_ref, g_ref, o_ref):
    x = x_ref[...]
    o_ref[...] = x * lax.rsqrt(jnp.mean(x * x, axis=-1, keepdims=True) + NORM_EPS) * g_ref[...]


def final_norm(x, g, *, ts):
    T, D = x.shape
    ts = _tile(T, ts)
    return pl.pallas_call(
        _final_norm_kernel, grid=(T // ts,),
        in_specs=[pl.BlockSpec((ts, D), lambda i: (i, 0)), pl.BlockSpec((1, D), lambda i: (0, 0))],
        out_specs=pl.BlockSpec((ts, D), lambda i: (i, 0)),
        out_shape=jax.ShapeDtypeStruct((T, D), F32),
        compiler_params=_params(1),
    )(x, g)


def dft_tables(S, C):
    r = int(round(math.sqrt(S)))
    assert r * r == S
    k = jnp.arange(S, dtype=jnp.int32)[:, None]
    n = jnp.arange(r, dtype=jnp.int32)[None, :]
    ang_hi = ((k * n * r) % S).astype(F32) * (2.0 * math.pi / S)
    ang_lo = ((k * n) % S).astype(F32) * (2.0 * math.pi / S)
    ch, sh_ = jnp.cos(ang_hi)[:, :, None], jnp.sin(ang_hi)[:, :, None]
    cl, sl = jnp.cos(ang_lo)[:, None, :], jnp.sin(ang_lo)[:, None, :]
    scale = float(S) ** -0.5
    cos_s = ((ch * cl - sh_ * sl) * scale).reshape(S, S)
    sin_s = ((sh_ * cl + ch * sl) * scale).reshape(S, S)
    m_seq = jnp.concatenate([cos_s, sin_s], axis=0).astype(BF16)
    kc = jnp.arange(C, dtype=jnp.int32)
    ang_c = ((kc[:, None] * kc[None, :]) % C).astype(F32) * (2.0 * math.pi / C)
    cscale = float(C) ** -0.5
    m_ch = jnp.concatenate([jnp.cos(ang_c) * cscale, -jnp.sin(ang_c) * cscale], axis=0).astype(BF16)
    return m_seq, m_ch


def _dft_seq_kernel(m_ref, h_ref, o_ref):
    o_ref[0] = jnp.dot(m_ref[...], h_ref[0], preferred_element_type=F32).astype(o_ref.dtype)


def dft_seq(m_seq, h3, *, tm):
    B, S, D = h3.shape
    C = D // FOURIER_GROUPS
    tm = _tile(S, tm)
    n_i = S // tm
    return pl.pallas_call(
        _dft_seq_kernel,
        grid=(2 * n_i, B, FOURIER_GROUPS),
        in_specs=[pl.BlockSpec((tm, S), lambda i, b, g: (i, 0)),
                  pl.BlockSpec((1, S, C), lambda i, b, g: (b, 0, g))],
        out_specs=pl.BlockSpec((1, tm, C), lambda i, b, g: (b, i % n_i, 2 * g + i // n_i)),
        out_shape=jax.ShapeDtypeStruct((B, S, 2 * D), BF16),
        compiler_params=_params(3),
    )(m_seq, h3)


def _mm_const_kernel(a_ref, w_ref, o_ref):
    o_ref[...] = jnp.dot(a_ref[...], w_ref[...], preferred_element_type=F32).astype(o_ref.dtype)


def dft_channel(z, m_ch, *, tm):
    T = z.shape[0]
    C2, C = m_ch.shape
    tm = _tile(T, tm)
    return pl.pallas_call(
        _mm_const_kernel,
        grid=(T // tm, FOURIER_GROUPS),
        in_specs=[pl.BlockSpec((tm, C2), lambda i, g: (i, g)),
                  pl.BlockSpec((C2, C), lambda i, g: (0, 0))],
        out_specs=pl.BlockSpec((tm, C), lambda i, g: (i, g)),
        out_shape=jax.ShapeDtypeStruct((T, FOURIER_GROUPS * C), BF16),
        compiler_params=_params(2),
    )(z, m_ch)


def _band_attn_kernel(q_ref, k_ref, v_ref, o_ref, lse_ref, *, L, tq, win, half, heads):
    hb = pl.program_id(2)
    scale = 1.0 / math.sqrt(HEAD_DIM)
    lane_head = lax.broadcasted_iota(jnp.int32, (tq, LANES), 1) // LSE_REP

    @pl.when(hb == 0)
    def _():
        lse_ref[...] = jnp.zeros(lse_ref.shape, lse_ref.dtype)

    def body(qi, carry):
        q0 = pl.multiple_of(qi * tq, tq)
        ks = pl.multiple_of(jnp.clip(q0 - half, 0, L - win), half)
        qpos = q0 + lax.broadcasted_iota(jnp.int32, (tq, win), 0)
        kpos = ks + lax.broadcasted_iota(jnp.int32, (tq, win), 1)
        band = jnp.abs(qpos - kpos) <= half
        lse_tile = lse_ref[0, 0, pl.ds(q0, tq), :]
        for h in range(heads):
            cs = slice(h * HEAD_DIM, (h + 1) * HEAD_DIM)
            q = q_ref[0, 0, pl.ds(q0, tq), cs]
            k = k_ref[0, 0, pl.ds(ks, win), cs]
            v = v_ref[0, 0, pl.ds(ks, win), cs]
            s = lax.dot_general(q, k, (((1,), (1,)), ((), ())), preferred_element_type=F32) * scale
            s = jnp.where(band, s, NEG_INF)
            m = jnp.max(s, axis=-1, keepdims=True)
            p = jnp.exp(s - m)
            denom = jnp.sum(p, axis=-1, keepdims=True)
            o = jnp.dot(p.astype(BF16), v, preferred_element_type=F32) / denom
            o_ref[0, 0, pl.ds(q0, tq), cs] = o.astype(o_ref.dtype)
            lse_tile = jnp.where(lane_head == hb * heads + h, m + jnp.log(denom), lse_tile)
        lse_ref[0, 0, pl.ds(q0, tq), :] = lse_tile
        return carry

    n_blocks = L // tq
    lax.fori_loop(0, n_blocks, body, 0, unroll=max(1, min(n_blocks, ATTN_CHAINS // heads)))


ATTN_BLOCK_BYTES = 4 * 1024 * 1024
ATTN_CHAINS = 16


def band_attention(qkv, dilation, half, *, B, S):
    L = S // dilation
    H = HEADS_PER_GROUP
    tq = min(2 * half, L)
    win = min(tq + 2 * half, L)
    heads_per_step = max(n for n in (1, 2, 4, 8) if H % n == 0 and (n == 1 or L * n * HEAD_DIM * 2 <= ATTN_BLOCK_BYTES))
    hw = heads_per_step * HEAD_DIM
    n_hb = H // heads_per_step

    def spec(which):
        return pl.BlockSpec((1, 1, L, hw), lambda b, r, h: (r, b, 0, which * n_hb + h))

    return pl.pallas_call(
        functools.partial(_band_attn_kernel, L=L, tq=tq, win=win, half=half, heads=heads_per_step),
        grid=(B, dilation, n_hb),
        in_specs=[spec(0), spec(1), spec(2)],
        out_specs=[pl.BlockSpec((1, 1, L, hw), lambda b, r, h: (r, b, 0, h)),
                   pl.BlockSpec((1, 1, L, LANES), lambda b, r, h: (r, b, 0, 0))],
        out_shape=[jax.ShapeDtypeStruct((dilation, B, L, H * HEAD_DIM), BF16),
                   jax.ShapeDtypeStruct((dilation, B, L, LANES), F32)],
        compiler_params=_params(3),
    )(qkv, qkv, qkv)


def _attn_merge_kernel(*refs, dilations):
    n = len(dilations)
    o_refs, l_refs, out_ref = refs[:n], refs[n:2 * n], refs[2 * n]
    scratch = refs[2 * n + 1:]
    tm = out_ref.shape[0]
    outs, lses, k = [], [], 0
    for o_ref, l_ref, d in zip(o_refs, l_refs, dilations):
        if d == 1:
            outs.append(lambda h, o_ref=o_ref: o_ref[0, 0, :, h * HEAD_DIM:(h + 1) * HEAD_DIM].astype(F32))
            lses.append(l_ref[0, 0])
            continue
        so_ref, sl_ref = scratch[k], scratch[k + 1]
        k += 2
        for r in range(d):
            sl_ref[0, pl.ds(r, tm // d, stride=d), :] = l_ref[r, 0]
            for h in range(HEADS_PER_GROUP):
                so_ref[h, pl.ds(r, tm // d, stride=d), :] = (
                    o_ref[r, 0, :, h * HEAD_DIM:(h + 1) * HEAD_DIM].astype(F32))
        outs.append(lambda h, so_ref=so_ref: so_ref[h])
        lses.append(sl_ref[0])
    m = functools.reduce(jnp.maximum, lses)
    es = [jnp.exp(l - m) for l in lses]
    inv = 1.0 / functools.reduce(lambda a, b: a + b, es)
    ws = [e * inv for e in es]
    for h in range(HEADS_PER_GROUP):
        c = h * LSE_REP
        acc = functools.reduce(lambda a, b: a + b, [w[:, c:c + 1] * o(h) for w, o in zip(ws, outs)])
        out_ref[:, h * HEAD_DIM:(h + 1) * HEAD_DIM] = acc.astype(out_ref.dtype)


def attn_merge(outs, lses, dilations, *, B, S, tm):
    W = outs[0].shape[3]
    tm = _tile(S, tm)
    per_seq = S // tm
    o_specs = [pl.BlockSpec((d, 1, tm // d, W), lambda b, i: (0, b, i, 0)) for d in dilations]
    l_specs = [pl.BlockSpec((d, 1, tm // d, LANES), lambda b, i: (0, b, i, 0)) for d in dilations]
    scratch = []
    for d in dilations:
        if d > 1:
            scratch += [pltpu.VMEM((W // HEAD_DIM, tm, HEAD_DIM), F32), pltpu.VMEM((1, tm, LANES), F32)]
    return pl.pallas_call(
        functools.partial(_attn_merge_kernel, dilations=dilations), grid=(B, per_seq),
        in_specs=o_specs + l_specs,
        out_specs=pl.BlockSpec((tm, W), lambda b, i: (b * per_seq + i, 0)),
        out_shape=jax.ShapeDtypeStruct((B * S, W), BF16),
        scratch_shapes=scratch,
        compiler_params=_params(2),
    )(*outs, *lses)


def _ffn_up_kernel(x_ref, wg_ref, wu_ref, o_ref):
    x = x_ref[0]
    g = jnp.dot(x, wg_ref[0, 0].astype(BF16), preferred_element_type=F32)
    u = jnp.dot(x, wu_ref[0, 0].astype(BF16), preferred_element_type=F32)
    o_ref[0] = (g * jax.nn.sigmoid(g) * u).astype(o_ref.dtype)


def _ffn_down_kernel(h_ref, wd_ref, gate_ref, o_ref):
    acc = jnp.dot(h_ref[0], wd_ref[0, 0].astype(BF16), preferred_element_type=F32)
    o_ref[0] = (acc * gate_ref[0]).astype(o_ref.dtype)


def expert_ffn(xe, w_gate, w_up, w_down, layer, gates, *, tm, tf, td):
    E, C, D = xe.shape
    F = w_gate.shape[3]
    tm = _tile(C, tm)
    tf = _tile(F, tf)
    td = _tile(D, td)
    hid = pl.pallas_call(
        _ffn_up_kernel,
        grid=(E, C // tm, F // tf),
        in_specs=[pl.BlockSpec((1, tm, D), lambda e, i, f: (e, i, 0)),
                  pl.BlockSpec((1, 1, D, tf), lambda e, i, f: (layer, e, 0, f)),
                  pl.BlockSpec((1, 1, D, tf), lambda e, i, f: (layer, e, 0, f))],
        out_specs=pl.BlockSpec((1, tm, tf), lambda e, i, f: (e, i, f)),
        out_shape=jax.ShapeDtypeStruct((E, C, F), BF16),
        compiler_params=_params(3),
    )(xe, w_gate, w_up)
    return pl.pallas_call(
        _ffn_down_kernel,
        grid=(E, C // tm, D // td),
        in_specs=[pl.BlockSpec((1, tm, F), lambda e, i, j: (e, i, 0)),
                  pl.BlockSpec((1, 1, F, td), lambda e, i, j: (layer, e, 0, j)),
                  pl.BlockSpec((1, tm, 1), lambda e, i, j: (e, i, 0))],
        out_specs=pl.BlockSpec((1, tm, td), lambda e, i, j: (e, i, j)),
        out_shape=jax.ShapeDtypeStruct((E, C, D), BF16),
        compiler_params=_params(3),
    )(hid, w_down, gates)


COMBINE_TOKENS = 256
SLOT_BLOCK = 64
SLOT_WINDOW = 2 * SLOT_BLOCK


def _combine_kernel(base_ref, nr_ref, x_ref, gt_ref, pos_ref, *refs, n_exp):
    y_refs, o_ref, acc_ref = refs[:2 * n_exp], refs[2 * n_exp], refs[2 * n_exp + 1]
    tb, r = pl.program_id(0), pl.program_id(1)
    n_tok = x_ref.shape[0]

    @pl.when(r == 0)
    def _():
        acc_ref[...] = jnp.zeros(acc_ref.shape, acc_ref.dtype)

    @pl.when(r < nr_ref[tb])
    def _():
        lane = lax.broadcasted_iota(jnp.int32, (n_tok, SLOT_WINDOW), 1)
        pos = pos_ref[...]
        onehots = []
        for e in range(n_exp):
            start = base_ref[tb * n_exp + e] + r * SLOT_WINDOW
            onehots.append(jnp.where(pos[:, e:e + 1] - start == lane, 1.0, 0.0).astype(BF16))
        onehot = jnp.concatenate(onehots, axis=1)
        rows = jnp.concatenate([y[0] for y in y_refs], axis=0)
        acc_ref[...] += jnp.dot(onehot, rows, preferred_element_type=F32)

    @pl.when(r == pl.num_programs(1) - 1)
    def _():
        o_ref[...] = x_ref[...] + gt_ref[0] * acc_ref[...]


def moe_combine(x, gate, ye, pos_tm, base, n_rounds, *, rows_per_seq):
    T, D = x.shape
    E, C, _ = ye.shape
    tb = COMBINE_TOKENS
    per_seq = rows_per_seq // tb
    max_rounds = -(-(SLOT_BLOCK - 1 + tb) // SLOT_WINDOW)
    last_blk = C // SLOT_BLOCK - 1

    def y_spec(e, k):
        def imap(t, r, base_ref, nr_ref):
            rr = jnp.minimum(r, nr_ref[t] - 1)
            blk = base_ref[t * E + e] // SLOT_BLOCK + 2 * rr + k
            return (e, jnp.minimum(blk, last_blk), 0)
        return pl.BlockSpec((1, SLOT_BLOCK, D), imap)

    grid_spec = pltpu.PrefetchScalarGridSpec(
        num_scalar_prefetch=2,
        grid=(T // tb, max_rounds),
        in_specs=[pl.BlockSpec((tb, D), lambda t, r, b, n: (t, 0)),
                  pl.BlockSpec((1, 1, D), lambda t, r, b, n: (t // per_seq, 0, 0)),
                  pl.BlockSpec((tb, LANES), lambda t, r, b, n: (t, 0))]
                 + [y_spec(e, k) for e in range(E) for k in range(2)],
        out_specs=pl.BlockSpec((tb, D), lambda t, r, b, n: (t, 0)),
        scratch_shapes=[pltpu.VMEM((tb, D), F32)],
    )
    return pl.pallas_call(
        functools.partial(_combine_kernel, n_exp=E),
        grid_spec=grid_spec,
        out_shape=jax.ShapeDtypeStruct((T, D), F32),
        compiler_params=_params(2),
    )(base, n_rounds, x, gate, pos_tm, *([ye] * (2 * E)))


ROUTE_BLOCK = 512


def _route_threshold_kernel(aff_ref, thr_ref, need_ref, *, cap):
    bits = pltpu.bitcast(aff_ref[...], jnp.int32)

    def step(i, v):
        cand = v | (1 << (30 - i))
        cnt = jnp.sum(jnp.where(bits >= cand, 1, 0), axis=1, keepdims=True)
        return jnp.where(cnt >= cap, cand, v)

    v = lax.fori_loop(0, 31, step, jnp.zeros((bits.shape[0], 1), jnp.int32))
    thr_ref[...] = v
    need_ref[...] = cap - jnp.sum(jnp.where(bits > v, 1, 0), axis=1, keepdims=True)


def _route_scan_kernel(aff_ref, thr_ref, need_ref, tri_ref, pos_ref, csum_ref, tied_acc, sel_acc):
    @pl.when(pl.program_id(0) == 0)
    def _():
        tied_acc[...] = jnp.zeros(tied_acc.shape, tied_acc.dtype)
        sel_acc[...] = jnp.zeros(sel_acc.shape, sel_acc.dtype)

    bits = pltpu.bitcast(aff_ref[...], jnp.int32)
    thr = thr_ref[...]
    tri = tri_ref[...]
    tied = bits == thr
    tied_count = tied_acc[...] + jnp.dot(jnp.where(tied, 1.0, 0.0).astype(BF16), tri, preferred_element_type=F32)
    sel = (bits > thr) | (tied & (tied_count.astype(jnp.int32) - 1 < need_ref[...]))
    sel_count = sel_acc[...] + jnp.dot(jnp.where(sel, 1.0, 0.0).astype(BF16), tri, preferred_element_type=F32)
    csum = sel_count.astype(jnp.int32)
    pos_ref[...] = jnp.where(sel, csum - 1, -1)
    csum_ref[...] = csum
    tied_acc[...] = tied_count[:, -1:]
    sel_acc[...] = sel_count[:, -1:]


def _route(logits, cap):
    aff_t = jax.nn.softmax(logits[:, :N_EXPERTS], axis=-1).T
    E, T = aff_t.shape
    thr, need = pl.pallas_call(
        functools.partial(_route_threshold_kernel, cap=cap),
        out_shape=[jax.ShapeDtypeStruct((E, 1), jnp.int32)] * 2,
        compiler_params=pltpu.CompilerParams(vmem_limit_bytes=VMEM_LIMIT),
    )(aff_t)
    rb = _tile(T, ROUTE_BLOCK)
    tri = (jnp.arange(rb)[:, None] <= jnp.arange(rb)[None, :]).astype(BF16)
    col = pl.BlockSpec((E, 1), lambda i: (0, 0))
    blk = pl.BlockSpec((E, rb), lambda i: (0, i))
    pos, csum = pl.pallas_call(
        _route_scan_kernel,
        grid=(T // rb,),
        in_specs=[blk, col, col, pl.BlockSpec((rb, rb), lambda i: (0, 0))],
        out_specs=[blk, blk],
        out_shape=[jax.ShapeDtypeStruct((E, T), jnp.int32)] * 2,
        scratch_shapes=[pltpu.VMEM((E, 1), F32), pltpu.VMEM((E, 1), F32)],
        compiler_params=_params(1),
    )(aff_t, thr, need, tri)
    slots = jnp.arange(cap, dtype=jnp.int32)
    group_end = csum[:, LANES - 1::LANES]
    grp = jnp.sum(group_end[:, None, :] <= slots[None, :, None], axis=-1, dtype=jnp.int32)
    pick = (grp[:, :, None] == jnp.arange(T // LANES, dtype=jnp.int32)[None, None, :]).astype(F32)
    rows = jnp.einsum("esg,egl->esl", pick, csum.reshape(E, T // LANES, LANES).astype(F32),
                      precision=lax.Precision.HIGHEST)
    idx = grp * LANES + jnp.sum(rows <= slots[None, :, None].astype(F32), axis=-1, dtype=jnp.int32)
    gates = jnp.take_along_axis(aff_t, idx, axis=1)
    return idx, gates, pos, csum


def _moe(x, gate, h, logits, w_gate, w_up, w_down, layer, *, rows_per_seq):
    T, D = h.shape
    cap = EC_CAPACITY_FACTOR * T // N_EXPERTS
    idx, gates, pos, csum = _route(logits, cap)
    xe = h[idx]
    ye = expert_ffn(xe, w_gate, w_up, w_down, layer, gates[..., None], tm=2048, tf=512, td=256)
    n_tb = T // COMBINE_TOKENS
    ends = csum[:, COMBINE_TOKENS - 1::COMBINE_TOKENS].T
    starts = jnp.concatenate([jnp.zeros((1, N_EXPERTS), jnp.int32), ends[:-1]], axis=0)
    base = (starts // SLOT_BLOCK) * SLOT_BLOCK
    n_rounds = jnp.maximum(1, jnp.max(-(-(ends - base) // SLOT_WINDOW), axis=1)).astype(jnp.int32)
    pos_tm = jnp.pad(pos.T, ((0, 0), (0, LANES - N_EXPERTS)), constant_values=-1)
    return moe_combine(x, gate, ye, pos_tm, base.reshape(n_tb * N_EXPERTS), n_rounds,
                       rows_per_seq=rows_per_seq)


def _ada_modulation(c_all, w_ada, b_ada):
    n = c_all.shape[0]
    pad = (-n) % 16
    a = jnp.pad(jax.nn.silu(c_all), ((0, pad), (0, 0))).astype(BF16)
    return matmul_bias_layers(a, w_ada, b_ada[:, None, :], tn=1024)[:, :n]


def _run_trunk(x, mods, g_mix, g_ffn, w_fourier_out, w_qkv, w_attn_out, w_router_pad,
               w_gate, w_up, w_down, g_final, tables):
    B, S, D = x.shape
    T = B * S
    m_seq, m_ch, rope = tables
    x = x.reshape(T, D)
    depth = g_mix.shape[0]
    for i in range(depth):
        sh1, sc1, gt1, sh2, sc2, gt2 = [t[:, None, :] for t in jnp.split(mods[i], 6, axis=-1)]
        if i % N_MIXERS == 0:
            h = norm_modulate(x, g_mix[i][None, :], sh1, sc1, rows_per_seq=S, ts=512)
            z = dft_seq(m_seq, h.reshape(B, S, D), tm=1024)
            mixed = dft_channel(z.reshape(T, 2 * D), m_ch, tm=1024)
            x = matmul_residual(mixed, w_fourier_out, i // N_MIXERS, x, gt1, rows_per_seq=S, tm=2048, tn=512)
        else:
            dilations = tuple(d for _, d in DILATION_GROUPS)
            hvs = norm_modulate_classes(x, g_mix[i][None, :], sh1, sc1, dilations, rows_per_seq=S, ts=512)
            outs, lses = [], []
            for g, (window, dilation) in enumerate(DILATION_GROUPS):
                L = S // dilation
                qkv = matmul_qkv_rope(hvs[g], w_qkv, i // N_MIXERS, g, dilation, *rope,
                                      rows_per_class=L, tm=2048, tn=512)
                o, lse = band_attention(qkv.reshape(dilation, B, L, qkv.shape[2]), dilation,
                                        window // (2 * dilation), B=B, S=S)
                outs.append(o)
                lses.append(lse)
            merged = attn_merge(outs, lses, dilations, B=B, S=S, tm=512)
            x = matmul_residual(merged, w_attn_out, i // N_MIXERS, x, gt1, rows_per_seq=S, tm=2048, tn=512)
        h, logits = norm_modulate(x, g_ffn[i][None, :], sh2, sc2, rows_per_seq=S, ts=512,
                                  w_router=w_router_pad[i])
        x = _moe(x, gt2, h, logits, w_gate, w_up, w_down, i, rows_per_seq=S)
    return final_norm(x, g_final[None, :], ts=512).reshape(B, S, D)


def kernel(x_prompt, x_sample, c_prompt, c_sample, g_mix, g_ffn, w_ada, b_ada, w_fourier_out, w_qkv,
           w_attn_out, w_router, w_gate, w_up, w_down, g_final):
    nb = x_prompt.shape[0]
    mods = _ada_modulation(jnp.concatenate([c_prompt, c_sample], axis=0), w_ada, b_ada)
    w_router_pad = jnp.pad(w_router, ((0, 0), (0, 0), (0, LANES - w_router.shape[2]))).astype(BF16)
    w_qkv = permute_qk_columns(w_qkv)
    w_fourier_out = w_fourier_out.astype(BF16)
    w_attn_out = w_attn_out.astype(BF16)
    outs = []
    table_cache = {}
    for x, lo, hi in ((x_prompt, 0, nb), (x_sample, nb, nb + x_sample.shape[0])):
        S, D = x.shape[1], x.shape[2]
        if S not in table_cache:
            table_cache[S] = dft_tables(S, D // FOURIER_GROUPS) + (rope_tables(S),)
        tables = table_cache[S]
        outs.append(_run_trunk(x, mods[:, lo:hi], g_mix, g_ffn, w_fourier_out, w_qkv,
                               w_attn_out, w_router_pad, w_gate, w_up, w_down, g_final, tables))
    return tuple(outs)
```
